```python
import math
import jax
import jax.numpy as jnp
from jax import lax
import numpy as np

D_MODEL = 1024
BATCH = 4
SEQ = 8192
DEPTH = 1
DEC_BATCH = 32
DEC_SEQ = 1
PAST_LEN = 16384
PAGE_SIZE = 128

HEAD_DIM = 64
N_DIFF_HEADS = 8
N_FOX_HEADS = 8
DIFF_QK_DIM = HEAD_DIM // 2
DIFF_W = N_DIFF_HEADS * HEAD_DIM
FOX_W = N_FOX_HEADS * HEAD_DIM
MIX_WIDTH = DIFF_W + FOX_W
IN_WIDTH = 3 * DIFF_W + 3 * FOX_W + N_FOX_HEADS
D_FF = 2816
PLE_DIM = 256
Q_BLOCK = 128
RMS_EPS = 1e-6

kernel_name = 'hymba_diff_fox_macaron_step'


def _rmsnorm(x, g):
    xf = x.astype(jnp.float32)
    y = xf * lax.rsqrt(jnp.mean(xf * xf, axis=-1, keepdims=True) + RMS_EPS)
    return (y * g.astype(jnp.float32)).astype(x.dtype)


def _swiglu(x, w_gate, w_up, w_down):
    return (jax.nn.silu(x @ w_gate) * (x @ w_up)) @ w_down


def _lambda_init(layer):
    return 0.8 - 0.6 * math.exp(-0.3 * layer)


def _alibi_slopes(n_heads):
    return 2.0 ** (-8.0 * jnp.arange(1, n_heads + 1, dtype=jnp.float32) / n_heads)


def _alibi_bias(q_pos, k_pos, slopes):
    dist = (q_pos[:, None] - k_pos[None, :]).astype(jnp.float32)
    return jnp.where(dist >= 0, -slopes[:, None, None] * dist, -jnp.inf)


def _project(hn, w_in, b_forget):
    b, t, _ = hn.shape
    z = hn @ w_in
    def heads(lo, n):
        return z[..., lo:lo + n * HEAD_DIM].reshape(b, t, n, HEAD_DIM)
    qd = heads(0, N_DIFF_HEADS)
    kd = heads(DIFF_W, N_DIFF_HEADS)
    vd = heads(2 * DIFF_W, N_DIFF_HEADS)
    qf = heads(3 * DIFF_W, N_FOX_HEADS)
    kf = heads(3 * DIFF_W + FOX_W, N_FOX_HEADS)
    vf = heads(3 * DIFF_W + 2 * FOX_W, N_FOX_HEADS)
    f_logit = z[..., 3 * DIFF_W + 3 * FOX_W:] + b_forget
    logf = jax.nn.log_sigmoid(f_logit.astype(jnp.float32))
    return qd, kd, vd, qf, kf, vf, logf


def _diff_block(qd, q_pos, segs, lam):
    scale = DIFF_QK_DIM ** -0.5
    slopes = _alibi_slopes(N_DIFF_HEADS)
    def scores(lo, hi):
        return jnp.concatenate([
            jnp.einsum('bqhd,bkhd->bhqk', qd[..., lo:hi], kd[..., lo:hi],
                       preferred_element_type=jnp.float32) for kd, _, _ in segs], axis=-1)
    bias = jnp.concatenate([_alibi_bias(q_pos, kp, slopes) for _, _, kp in segs], axis=-1)
    a1 = jax.nn.softmax(scores(0, DIFF_QK_DIM) * scale + bias, axis=-1)
    a2 = jax.nn.softmax(scores(DIFF_QK_DIM, HEAD_DIM) * scale + bias, axis=-1)
    w = a1 - lam * a2
    out = 0.0
    start = 0
    for _, vd, kp in segs:
        n = kp.shape[0]
        out = out + jnp.einsum('bhqk,bkhd->bqhd', w[..., start:start + n].astype(vd.dtype), vd)
        start += n
    return out


def _fox_block(qf, cq, q_pos, segs):
    scale = HEAD_DIM ** -0.5
    s = jnp.concatenate([
        jnp.einsum('bqhd,bkhd->bhqk', qf, kf, preferred_element_type=jnp.float32)
        for kf, _, _, _ in segs], axis=-1) * scale
    cq_t = jnp.swapaxes(cq, 1, 2)[..., :, None]
    decay = jnp.concatenate([cq_t - jnp.swapaxes(ck, 1, 2)[..., None, :] for _, _, ck, _ in segs], axis=-1)
    mask = jnp.concatenate([q_pos[:, None] >= kp[None, :] for _, _, _, kp in segs], axis=-1)
    a = jax.nn.softmax(jnp.where(mask, s + decay, -jnp.inf), axis=-1)
    out = 0.0
    start = 0
    for _, vf, _, kp in segs:
        n = kp.shape[0]
        out = out + jnp.einsum('bhqk,bkhd->bqhd', a[..., start:start + n].astype(vf.dtype), vf)
        start += n
    return out


def _sweep_queries(fn, q_args, q_pos):
    n_q = q_pos.shape[0]
    if n_q <= Q_BLOCK or n_q % Q_BLOCK != 0:
        return fn(q_args, q_pos)
    nb = n_q // Q_BLOCK
    def blk(a):
        return jnp.moveaxis(a.reshape(a.shape[0], nb, Q_BLOCK, *a.shape[2:]), 1, 0)
    out = lax.map(lambda xs: fn(xs[0], xs[1]),
                  (tuple(blk(a) for a in q_args), q_pos.reshape(nb, Q_BLOCK)))
    out = jnp.moveaxis(out, 0, 1)
    return out.reshape(out.shape[0], n_q, *out.shape[3:])


def _attend_prompt(l, qd, kd, vd, qf, kf, vf, logf, lam):
    t = qd.shape[1]
    pos = jnp.arange(t, dtype=jnp.int32)
    c = lax.cumsum(logf, axis=1)
    o_d = _sweep_queries(lambda qa, qp: _diff_block(qa[0], qp, [(kd, vd, pos)], lam), (qd,), pos)
    o_f = _sweep_queries(lambda qa, qp: _fox_block(qa[0], qa[1], qp, [(kf, vf, c, pos)]), (qf, c), pos)
    return o_d, o_f


def _run_group(x, p, attend, w):
    h = x
    rows = []
    for l in range(DEPTH):
        lam0 = _lambda_init(l)
        h = h + 0.5 * _swiglu(_rmsnorm(h, w['ffn1_norm'][l]), w['ffn1_w_gate'][l],
                              w['ffn1_w_up'][l], w['ffn1_w_down'][l])
        hn = _rmsnorm(h, w['mix_norm'][l])
        qd, kd, vd, qf, kf, vf, logf = _project(hn, w['w_in'][l], w['b_forget'][l])
        f32 = jnp.float32
        lam = (jnp.exp(jnp.sum(w['lambda_q1'][l].astype(f32) * w['lambda_k1'][l].astype(f32)))
               - jnp.exp(jnp.sum(w['lambda_q2'][l].astype(f32) * w['lambda_k2'][l].astype(f32)))
               + lam0)
        o_d, o_f = attend(l, qd, kd, vd, qf, kf, vf, logf, lam)
        o_d = _rmsnorm(o_d, w['diff_subln'][l]) * (1.0 - lam0)
        b, t = h.shape[:2]
        o = jnp.concatenate([o_d.reshape(b, t, DIFF_W), o_f.reshape(b, t, FOX_W)], axis=-1)
        h = h + o @ w['w_out'][l]
        h = h + 0.5 * _swiglu(_rmsnorm(h, w['ffn2_norm'][l]), w['ffn2_w_gate'][l],
                              w['ffn2_w_up'][l], w['ffn2_w_down'][l])
        gate = jax.nn.sigmoid(_rmsnorm(h, w['ple_norm'][l]) @ w['w_ple_gate'][l] + w['b_ple_gate'][l])
        h = h + gate * (p[l] @ w['w_ple_proj'][l])
        rows.append((kd, vd, kf, vf, logf.astype(x.dtype)))
    y = _rmsnorm(h, w['final_norm'])
    new = tuple(jnp.stack([r[i] for r in rows]) for i in range(5))
    return y, new


def setup_inputs(seed: int = 0) -> dict:
    key = jax.random.key(seed)
    ks = iter(jax.random.split(key, 48))
    f32 = jnp.float32
    def nrm(shape, scale):
        return jax.random.normal(next(ks), shape, f32) * scale
    def gain(shape):
        return 1.0 + 0.02 * jax.random.normal(next(ks), shape, f32)
    n_pages = PAST_LEN // PAGE_SIZE
    n_used = DEC_BATCH * n_pages
    n_phys = n_used + (n_used + 3) // 4
    kv_shape = (DEPTH, n_phys, PAGE_SIZE, N_DIFF_HEADS, HEAD_DIM)
    fkv_shape = (DEPTH, n_phys, PAGE_SIZE, N_FOX_HEADS, HEAD_DIM)
    return {
        'x_prompt': nrm((BATCH, SEQ, D_MODEL), 1.0),
        'x_sample': nrm((DEC_BATCH, DEC_SEQ, D_MODEL), 1.0),
        'cache_diff_k': nrm(kv_shape, 1.0),
        'cache_diff_v': nrm(kv_shape, 1.0),
        'cache_fox_k': nrm(fkv_shape, 1.0),
        'cache_fox_v': nrm(fkv_shape, 1.0),
        'cache_fox_logf': jax.nn.log_sigmoid(2.5 + nrm((DEPTH, n_phys, PAGE_SIZE, N_FOX_HEADS), 1.0)),
        'page_table': jax.random.permutation(next(ks), n_phys)[:n_used].reshape(DEC_BATCH, n_pages).astype(jnp.int32),
        'p_prompt': nrm((DEPTH, BATCH, SEQ, PLE_DIM), 1.0),
        'p_sample': nrm((DEPTH, DEC_BATCH, DEC_SEQ, PLE_DIM), 1.0),
        'ffn1_norm': gain((DEPTH, D_MODEL)),
        'ffn1_w_gate': nrm((DEPTH, D_MODEL, D_FF), D_MODEL ** -0.5),
        'ffn1_w_up': nrm((DEPTH, D_MODEL, D_FF), D_MODEL ** -0.5),
        'ffn1_w_down': nrm((DEPTH, D_FF, D_MODEL), D_FF ** -0.5),
        'mix_norm': gain((DEPTH, D_MODEL)),
        'w_in': nrm((DEPTH, D_MODEL, IN_WIDTH), D_MODEL ** -0.5),
        'b_forget': jax.random.uniform(next(ks), (DEPTH, N_FOX_HEADS), f32, 1.0, 4.0),
        'lambda_q1': nrm((DEPTH, DIFF_QK_DIM), 0.1),
        'lambda_k1': nrm((DEPTH, DIFF_QK_DIM), 0.1),
        'lambda_q2': nrm((DEPTH, DIFF_QK_DIM), 0.1),
        'lambda_k2': nrm((DEPTH, DIFF_QK_DIM), 0.1),
        'diff_subln': gain((DEPTH, HEAD_DIM)),
        'w_out': nrm((DEPTH, MIX_WIDTH, D_MODEL), MIX_WIDTH ** -0.5),
        'ffn2_norm': gain((DEPTH, D_MODEL)),
        'ffn2_w_gate': nrm((DEPTH, D_MODEL, D_FF), D_MODEL ** -0.5),
        'ffn2_w_up': nrm((DEPTH, D_MODEL, D_FF), D_MODEL ** -0.5),
        'ffn2_w_down': nrm((DEPTH, D_FF, D_MODEL), D_FF ** -0.5),
        'ple_norm': gain((DEPTH, D_MODEL)),
        'w_ple_gate': nrm((DEPTH, D_MODEL, D_MODEL), D_MODEL ** -0.5),
        'b_ple_gate': nrm((DEPTH, D_MODEL), 0.02),
        'w_ple_proj': nrm((DEPTH, PLE_DIM, D_MODEL), PLE_DIM ** -0.5),
        'final_norm': gain((D_MODEL,)),
    }


def reference(x_prompt, x_sample, cache_diff_k, cache_diff_v, cache_fox_k, cache_fox_v, cache_fox_logf,
              page_table, p_prompt, p_sample, ffn1_norm, ffn1_w_gate, ffn1_w_up, ffn1_w_down, mix_norm,
              w_in, b_forget, lambda_q1, lambda_k1, lambda_q2, lambda_k2, diff_subln, w_out, ffn2_norm,
              ffn2_w_gate, ffn2_w_up, ffn2_w_down, ple_norm, w_ple_gate, b_ple_gate, w_ple_proj, final_norm):
    w = dict(ffn1_norm=ffn1_norm, ffn1_w_gate=ffn1_w_gate, ffn1_w_up=ffn1_w_up, ffn1_w_down=ffn1_w_down,
             mix_norm=mix_norm, w_in=w_in, b_forget=b_forget, lambda_q1=lambda_q1, lambda_k1=lambda_k1,
             lambda_q2=lambda_q2, lambda_k2=lambda_k2, diff_subln=diff_subln, w_out=w_out,
             ffn2_norm=ffn2_norm, ffn2_w_gate=ffn2_w_gate, ffn2_w_up=ffn2_w_up, ffn2_w_down=ffn2_w_down,
             ple_norm=ple_norm, w_ple_gate=w_ple_gate, b_ple_gate=b_ple_gate, w_ple_proj=w_ple_proj,
             final_norm=final_norm)

    def attend_sample(l, qd, kd, vd, qf, kf, vf, logf, lam):
        b, t = qd.shape[:2]
        past = page_table.shape[1] * PAGE_SIZE
        def gather(c):
            g = c[l][page_table]
            return g.reshape(b, past, *c.shape[3:])
        kd_p, vd_p = gather(cache_diff_k), gather(cache_diff_v)
        kf_p, vf_p = gather(cache_fox_k), gather(cache_fox_v)
        lf_p = gather(cache_fox_logf).astype(jnp.float32)
        ck_p = -(lax.cumsum(lf_p, axis=1, reverse=True) - lf_p)
        c_new = lax.cumsum(logf, axis=1)
        past_pos = jnp.arange(past, dtype=jnp.int32)
        new_pos = past + jnp.arange(t, dtype=jnp.int32)
        o_d = _sweep_queries(
            lambda qa, qp: _diff_block(qa[0], qp, [(kd_p, vd_p, past_pos), (kd, vd, new_pos)], lam),
            (qd,), new_pos)
        o_f = _sweep_queries(
            lambda qa, qp: _fox_block(qa[0], qa[1], qp, [(kf_p, vf_p, ck_p, past_pos), (kf, vf, c_new, new_pos)]),
            (qf, c_new), new_pos)
        return o_d, o_f

    y_prompt, (pk_d, pv_d, pk_f, pv_f, pl_f) = _run_group(x_prompt, p_prompt, _attend_prompt, w)
    y_sample, (sk_d, sv_d, sk_f, sv_f, sl_f) = _run_group(x_sample, p_sample, attend_sample, w)
    return (y_prompt, y_sample, pk_d, pv_d, pk_f, pv_f, pl_f, sk_d, sv_d, sk_f, sv_f, sl_f)
```

```python
import functools
import math

import numpy as np
import jax
import jax.numpy as jnp
from jax import lax
from jax.experimental import pallas as pl
from jax.experimental.pallas import tpu as pltpu

F32 = jnp.float32
BF16 = jnp.bfloat16

HEAD_DIM = 64
N_HEADS = 8
HEADS_W = N_HEADS * HEAD_DIM
DIFF_QK = HEAD_DIM // 2
LANES = 128
N_PAIRS = HEADS_W // LANES
AUG_PER_HEAD = 6
RMS_EPS = 1e-6
LOG2E = 1.4426950408889634
LAMBDA_INIT = 0.8 - 0.6 * math.exp(-0.3 * 0)
NEG_INF = float("-inf")
VMEM_LIMIT = 56 * 1024 * 1024


def _rms(x, g):
    return x * lax.rsqrt(jnp.mean(x * x, axis=-1, keepdims=True) + RMS_EPS) * g


def _dot(a, b):
    return jnp.dot(a, b, preferred_element_type=F32)


def _dot_nt(a, b):
    return lax.dot_general(a, b, (((1,), (1,)), ((), ())), preferred_element_type=F32)


def _split3(x):
    hi = x.astype(BF16).astype(F32)
    r = x - hi
    mid = r.astype(BF16).astype(F32)
    lo = (r - mid).astype(BF16).astype(F32)
    return hi, mid, lo


def _const_spec(shape):
    nd = len(shape)
    return pl.BlockSpec(shape, lambda *_: (0,) * nd, pipeline_mode=pl.Buffered(1))


def _swiglu(hn, wg_ref, wu_ref, wd_ref, chunk):
    d_ff = wg_ref.shape[1]
    out = None
    for c0 in range(0, d_ff, chunk):
        g = _dot(hn, wg_ref[:, c0:c0 + chunk])
        u = _dot(hn, wu_ref[:, c0:c0 + chunk])
        a = (g * jax.nn.sigmoid(g) * u).astype(BF16)
        part = _dot(a, wd_ref[c0:c0 + chunk, :])
        out = part if out is None else out + part
    return out


def _ff_chunk(d_ff):
    return d_ff


def _ffn_kernel(x_ref, g_ref, wg_ref, wu_ref, wd_ref, h_ref, *, chunk):
    x = x_ref[...]
    hn = _rms(x, g_ref[...]).astype(BF16)
    h_ref[...] = x + 0.5 * _swiglu(hn, wg_ref, wu_ref, wd_ref, chunk)


def _ffn(x, g, wg, wu, wd, tm):
    n, d = x.shape
    d_ff = wg.shape[1]
    return pl.pallas_call(
        functools.partial(_ffn_kernel, chunk=_ff_chunk(d_ff)),
        grid=(n // tm,),
        in_specs=[pl.BlockSpec((tm, d), lambda i: (i, 0)),
                  _const_spec((1, d)), _const_spec((d, d_ff)), _const_spec((d, d_ff)), _const_spec((d_ff, d))],
        out_specs=pl.BlockSpec((tm, d), lambda i: (i, 0)),
        out_shape=jax.ShapeDtypeStruct((n, d), F32),
        compiler_params=pltpu.CompilerParams(dimension_semantics=("arbitrary",), vmem_limit_bytes=VMEM_LIMIT),
        name="ffn1",
    )(x, g, wg, wu, wd)


def _proj_kernel(h_ref, g_ref, win_ref, wf_ref, bf_ref, tri_ref, selq_ref, selk_ref,
                 q_ref, k_ref, v_ref, kd_ref, vd_ref, kf_ref, vf_ref, lf_ref, qaug_ref, kaug_ref,
                 carry_ref, *, q_scale_d, q_scale_f):
    t = pl.program_id(1)
    w = HEADS_W

    @pl.when(t == 0)
    def _():
        carry_ref[...] = jnp.zeros_like(carry_ref)

    hn = _rms(h_ref[0], g_ref[...]).astype(BF16)
    z = _dot(hn, win_ref[...])
    kd, vd = z[:, w:2 * w], z[:, 2 * w:3 * w]
    kf, vf = z[:, 4 * w:5 * w], z[:, 5 * w:6 * w]
    kd_ref[0], vd_ref[0], kf_ref[0], vf_ref[0] = kd, vd, kf, vf
    q_ref[0, :, :w] = (z[:, :w] * q_scale_d).astype(BF16)
    q_ref[0, :, w:] = (z[:, 3 * w:4 * w] * q_scale_f).astype(BF16)
    k_ref[0, :, :w] = kd.astype(BF16)
    k_ref[0, :, w:] = kf.astype(BF16)
    v_ref[0, :, :w] = vd.astype(BF16)
    v_ref[0, :, w:] = vf.astype(BF16)

    x = _dot(hn, wf_ref[...]) + bf_ref[...]
    lane = lax.broadcasted_iota(jnp.int32, x.shape, 1)
    lf = -(jnp.maximum(-x, 0.0) + jnp.log1p(jnp.exp(-jnp.abs(x))))
    lf = jnp.where(lane < N_HEADS, lf, 0.0)
    lf_ref[0] = lf[:, :N_HEADS]

    hi, mid, lo = _split3(lf * LOG2E)
    tri = tri_ref[...]
    c = (_dot(tri, hi.astype(BF16)) + _dot(tri, mid.astype(BF16)) + _dot(tri, lo.astype(BF16))
         + carry_ref[...])
    tm = c.shape[0]
    carry_ref[...] = c[tm - 1:tm, :]

    chi, cmid, clo = _split3(c)
    packed = (chi + pltpu.roll(cmid, N_HEADS, 1) + pltpu.roll(clo, 2 * N_HEADS, 1)
              + jnp.where(lane == 3 * N_HEADS, 1.0, 0.0)).astype(BF16)
    qaug_ref[0] = _dot(packed, selq_ref[...]).astype(BF16)
    kaug_ref[0] = _dot(packed, selk_ref[...]).astype(BF16)


def _aug_select_matrices():
    selq = np.zeros((LANES, N_PAIRS * LANES), np.float32)
    selk = np.zeros((LANES, N_PAIRS * LANES), np.float32)
    one = 3 * N_HEADS
    for p in range(N_PAIRS):
        for s in range(2):
            head = 2 * p + s
            base = p * LANES + s * AUG_PER_HEAD
            for piece in range(3):
                selq[piece * N_HEADS + head, base + piece] = 1.0
                selq[one, base + 3 + piece] = 1.0
                selk[one, base + piece] = 1.0
                selk[piece * N_HEADS + head, base + 3 + piece] = -1.0
    return jnp.asarray(selq, BF16), jnp.asarray(selk, BF16)


def _project(h, g, w_in, w_f, b_f, tm):
    b, t, d = h.shape
    w = HEADS_W
    selq, selk = _aug_select_matrices()
    tri = jnp.asarray(np.tril(np.ones((tm, tm), np.float32)), BF16)
    row = lambda width: pl.BlockSpec((1, tm, width), lambda i, j: (i, j, 0))
    sds = lambda width, dt: jax.ShapeDtypeStruct((b, t, width), dt)
    kern = functools.partial(_proj_kernel,
                             q_scale_d=DIFF_QK ** -0.5 * LOG2E, q_scale_f=HEAD_DIM ** -0.5 * LOG2E)
    return pl.pallas_call(
        kern,
        grid=(b, t // tm),
        in_specs=[row(d), _const_spec((1, d)), _const_spec(w_in.shape), _const_spec(w_f.shape),
                  _const_spec((1, LANES)), _const_spec((tm, tm)), _const_spec(selq.shape), _const_spec(selk.shape)],
        out_specs=[row(2 * w), row(2 * w), row(2 * w), row(w), row(w), row(w), row(w),
                   row(N_HEADS), row(N_PAIRS * LANES), row(N_PAIRS * LANES)],
        out_shape=[sds(2 * w, BF16), sds(2 * w, BF16), sds(2 * w, BF16),
                   sds(w, F32), sds(w, F32), sds(w, F32), sds(w, F32),
                   sds(N_HEADS, F32), sds(N_PAIRS * LANES, BF16), sds(N_PAIRS * LANES, BF16)],
        scratch_shapes=[pltpu.VMEM((1, LANES), F32)],
        compiler_params=pltpu.CompilerParams(dimension_semantics=("arbitrary", "arbitrary"),
                                             vmem_limit_bytes=VMEM_LIMIT),
        name="project",
    )(h, g, w_in, w_f, b_f, tri, selq, selk)


def _alibi_kernel(q_ref, k_ref):
    rows = q_ref.shape[0]
    pos = (pl.program_id(0) * rows + lax.broadcasted_iota(jnp.int32, (rows, LANES), 0)).astype(F32)
    lane = lax.broadcasted_iota(jnp.int32, (rows, LANES), 1)
    hi, mid, lo = _split3(pos * LOG2E)
    pieces = jnp.where(lane % 3 == 0, hi, jnp.where(lane % 3 == 1, mid, lo))
    q_ref[...] = jnp.where(lane < 3, -pieces, jnp.where(lane < 6, 1.0, 0.0)).astype(BF16)
    k_ref[...] = jnp.where(lane < 3, 1.0, jnp.where(lane < 6, pieces, 0.0)).astype(BF16)


def _alibi_tables(n, rows):
    return pl.pallas_call(
        _alibi_kernel,
        grid=(n // rows,),
        out_specs=[pl.BlockSpec((rows, LANES), lambda i: (i, 0))] * 2,
        out_shape=[jax.ShapeDtypeStruct((n, LANES), BF16)] * 2,
        name="alibi_tables",
    )()


def _alibi_slopes():
    slopes = 2.0 ** (-8.0 * np.arange(1, N_HEADS + 1, dtype=np.float64) / N_HEADS)
    assert np.all(np.log2(slopes) == np.round(np.log2(slopes))), "bias columns assume power-of-two slopes"
    return slopes.astype(np.float32)


def _lambda_full(lam_ref):
    l = lam_ref[...]
    s1 = jnp.sum(l[0:1] * l[1:2], axis=-1, keepdims=True)
    s2 = jnp.sum(l[2:3] * l[3:4], axis=-1, keepdims=True)
    return jnp.exp(s1) - jnp.exp(s2) + LAMBDA_INIT


def _head_rms(o, subln_row, lane):
    sq = o * o
    first = lane < HEAD_DIM
    sum_a = jnp.sum(jnp.where(first, sq, 0.0), axis=-1, keepdims=True)
    sum_b = jnp.sum(jnp.where(first, 0.0, sq), axis=-1, keepdims=True)
    ms = jnp.where(first, sum_a, sum_b) * (1.0 / HEAD_DIM)
    return o * lax.rsqrt(ms + RMS_EPS) * subln_row


def _attn_kernel(slope_ref, q_ref, k_ref, v_ref, qaug_ref, kaug_ref, lam_ref, subln_ref, o_ref,
                 qs_ref, m_ref, acc_ref, *, is_diff, tq, tk):
    n_maps = 4 if is_diff else 2
    rows = n_maps * tq
    pair = pl.program_id(1)
    qi = pl.program_id(2)
    lane = lax.broadcasted_iota(jnp.int32, (tq, LANES), 1)

    q = q_ref[0].astype(F32)
    qa = qaug_ref[0].astype(F32)
    map_w = LANES // n_maps
    for m in range(n_maps):
        qm = jnp.where((lane >= m * map_w) & (lane < (m + 1) * map_w), q, 0.0)
        if is_diff:
            aug = qa * slope_ref[2 * pair + m // 2]
        else:
            aug = jnp.where((lane >= m * AUG_PER_HEAD) & (lane < (m + 1) * AUG_PER_HEAD), qa, 0.0)
        qs_ref[m * tq:(m + 1) * tq, :LANES] = qm.astype(BF16)
        qs_ref[m * tq:(m + 1) * tq, LANES:] = aug.astype(BF16)

    m_ref[...] = jnp.full_like(m_ref, NEG_INF)
    acc_ref[...] = jnp.zeros_like(acc_ref)
    ones = jnp.ones((tk, LANES), BF16)

    def step(j, masked):
        start = pl.multiple_of(j * tk, tk)
        kfull = jnp.concatenate([k_ref[0, pl.ds(start, tk), :], kaug_ref[0, pl.ds(start, tk), :]], axis=1)
        s = _dot_nt(qs_ref[...], kfull)
        if masked:
            r = lax.broadcasted_iota(jnp.int32, (rows, tk), 0) % tq + qi * tq
            c = lax.broadcasted_iota(jnp.int32, (rows, tk), 1) + start
            s = jnp.where(c <= r, s, NEG_INF)
        m_prev = m_ref[...]
        m_new = jnp.maximum(m_prev, jnp.max(s, axis=1, keepdims=True))
        p = jnp.exp2(s - jnp.concatenate([m_new] * (tk // LANES), axis=1)).astype(BF16)
        alpha = jnp.exp2(m_prev - m_new)
        vfull = jnp.concatenate([v_ref[0, pl.ds(start, tk), :], ones], axis=1)
        acc_ref[...] = acc_ref[...] * jnp.concatenate([alpha, alpha], axis=1) + _dot(p, vfull)
        m_ref[...] = m_new

    n_full = (qi * tq) // tk

    def body(j, carry):
        step(j, False)
        return carry

    lax.fori_loop(0, n_full, body, 0)
    for d in range(tq // tk):
        step(n_full + d, True)

    def normalized(m):
        a = acc_ref[m * tq:(m + 1) * tq, :]
        return a[:, :LANES] / a[:, LANES:]

    first = lane < HEAD_DIM
    if is_diff:
        lam = _lambda_full(lam_ref)
        o = jnp.where(first, normalized(0) - lam * normalized(1), normalized(2) - lam * normalized(3))
        o = _head_rms(o, subln_ref[...], lane) * (1.0 - LAMBDA_INIT)
    else:
        o = jnp.where(first, normalized(0), normalized(1))
    o_ref[0] = o.astype(o_ref.dtype)


def _attend_prompt(q, k, v, qaug, kaug, slopes, lam, subln, *, is_diff, tq, tk):
    b, t, _ = q.shape
    n_maps = 4 if is_diff else 2
    col0 = 0 if is_diff else N_PAIRS
    rows = n_maps * tq
    if is_diff:
        qaug_spec = pl.BlockSpec((1, tq, LANES), lambda i, p, j, s: (0, j, 0))
        kaug_spec = pl.BlockSpec((1, t, LANES), lambda i, p, j, s: (0, 0, 0))
    else:
        qaug_spec = pl.BlockSpec((1, tq, LANES), lambda i, p, j, s: (i, j, p))
        kaug_spec = pl.BlockSpec((1, t, LANES), lambda i, p, j, s: (i, 0, p))
    grid_spec = pltpu.PrefetchScalarGridSpec(
        num_scalar_prefetch=1,
        grid=(b, N_PAIRS, t // tq),
        in_specs=[pl.BlockSpec((1, tq, LANES), lambda i, p, j, s: (i, j, col0 + p)),
                  pl.BlockSpec((1, t, LANES), lambda i, p, j, s: (i, 0, col0 + p)),
                  pl.BlockSpec((1, t, LANES), lambda i, p, j, s: (i, 0, col0 + p)),
                  qaug_spec, kaug_spec,
                  pl.BlockSpec(lam.shape, lambda i, p, j, s: (0, 0)),
                  pl.BlockSpec(subln.shape, lambda i, p, j, s: (0, 0))],
        out_specs=pl.BlockSpec((1, tq, LANES), lambda i, p, j, s: (i, j, p)),
        scratch_shapes=[pltpu.VMEM((rows, 2 * LANES), BF16),
                        pltpu.VMEM((rows, LANES), F32),
                        pltpu.VMEM((rows, 2 * LANES), F32)],
    )
    return pl.pallas_call(
        functools.partial(_attn_kernel, is_diff=is_diff, tq=tq, tk=tk),
        grid_spec=grid_spec,
        out_shape=jax.ShapeDtypeStruct((b, t, HEADS_W), BF16),
        compiler_params=pltpu.CompilerParams(dimension_semantics=("arbitrary",) * 3,
                                             vmem_limit_bytes=VMEM_LIMIT),
        name="attn_diff" if is_diff else "attn_fox",
    )(slopes, q, k, v, qaug, kaug, lam, subln)


def _decode_kernel(pt_ref, qd_ref, qf_ref, knd_ref, vnd_ref, knf_ref, vnf_ref, lfn_ref, slope_ref,
                   lam_ref, subln_ref, wsuf_ref, *rest, pages_per_step, n_pages, page):
    g = pages_per_step
    dk_refs, dv_refs = rest[0:g], rest[g:2 * g]
    fk_refs, fv_refs, lf_refs = rest[2 * g:3 * g], rest[3 * g:4 * g], rest[4 * g:5 * g]
    od_ref, of_ref = rest[5 * g], rest[5 * g + 1]
    md_ref, ld_ref, accd_ref, mf_ref, lfs_ref, accf_ref, tail_ref = rest[5 * g + 2:]
    del pt_ref
    j = pl.program_id(1)
    nh = N_HEADS
    flat = page * nh
    lane64 = lax.broadcasted_iota(jnp.int32, (2 * nh, HEAD_DIM), 1)
    qd = qd_ref[...].astype(F32)
    qd2 = jnp.concatenate([qd, qd], axis=0)
    qd2 = jnp.where((lane64 < DIFF_QK) == (lax.broadcasted_iota(jnp.int32, (2 * nh, HEAD_DIM), 0) < nh),
                    qd2, 0.0)
    qf = qf_ref[...].astype(F32)

    @pl.when(j == 0)
    def _():
        knd = jnp.concatenate([knd_ref[...], knd_ref[...]], axis=0)
        md_ref[...] = jnp.sum(qd2 * knd, axis=-1, keepdims=True)
        ld_ref[...] = jnp.ones_like(ld_ref)
        accd_ref[...] = jnp.concatenate([vnd_ref[...], vnd_ref[...]], axis=0)
        mf_ref[...] = jnp.sum(qf * knf_ref[...], axis=-1, keepdims=True)
        lfs_ref[...] = jnp.ones_like(lfs_ref)
        accf_ref[...] = vnf_ref[...]
        tail_ref[...] = jnp.zeros_like(tail_ref)

    def online(s, m_ref, l_ref, acc_ref, v2):
        m_prev = m_ref[...]
        m_new = jnp.maximum(m_prev, jnp.max(s, axis=-1, keepdims=True))
        p = jnp.exp2(s - m_new)
        alpha = jnp.exp2(m_prev - m_new)
        l_ref[...] = alpha * l_ref[...] + jnp.sum(p, axis=-1, keepdims=True)
        acc_ref[...] = alpha * acc_ref[...] + _dot(p.astype(BF16), v2)
        m_ref[...] = m_new

    sub16 = lax.broadcasted_iota(jnp.int32, (2 * nh, flat), 0)
    lane16 = lax.broadcasted_iota(jnp.int32, (2 * nh, flat), 1)
    own16 = (lane16 % nh) == (sub16 % nh)
    sub8 = lax.broadcasted_iota(jnp.int32, (nh, flat), 0)
    lane8 = lax.broadcasted_iota(jnp.int32, (nh, flat), 1)
    own8 = (lane8 % nh) == sub8
    slope2 = jnp.concatenate([slope_ref[...], slope_ref[...]], axis=0)
    past = n_pages * page

    for i in range(g):
        logical = n_pages - 1 - (j * g + i)
        k2 = dk_refs[i][...].reshape(flat, HEAD_DIM).astype(BF16)
        v2 = dv_refs[i][...].reshape(flat, HEAD_DIM).astype(BF16)
        s = _dot_nt(qd2.astype(BF16), k2)
        dist = (past - (logical * page + lane16 // nh)).astype(F32)
        s = jnp.where(own16, s - slope2 * LOG2E * dist, NEG_INF)
        online(s, md_ref, ld_ref, accd_ref, v2)

        k2 = fk_refs[i][...].reshape(flat, HEAD_DIM).astype(BF16)
        v2 = fv_refs[i][...].reshape(flat, HEAD_DIM).astype(BF16)
        s = _dot_nt(qf.astype(BF16), k2)
        hi, mid, lo = _split3(lf_refs[i][...] * LOG2E)
        tn = lambda a: lax.dot_general(a.astype(BF16), wsuf_ref[...], (((0,), (0,)), ((), ())),
                                       preferred_element_type=F32)
        suf = tn(hi) + tn(mid) + tn(lo)
        tail = tail_ref[...]
        decay = suf[:, :flat] + jnp.concatenate([tail] * (flat // LANES), axis=1) + lfn_ref[...] * LOG2E
        tail_ref[...] = tail + suf[:, flat:]
        s = jnp.where(own8, s + decay, NEG_INF)
        online(s, mf_ref, lfs_ref, accf_ref, v2)

    @pl.when(j == pl.num_programs(1) - 1)
    def _():
        of_ref[...] = accf_ref[...] / lfs_ref[...]
        lam = _lambda_full(lam_ref)
        a = accd_ref[...] / ld_ref[...]
        o = a[:nh] - lam * a[nh:]
        ms = jnp.mean(o * o, axis=-1, keepdims=True)
        od_ref[...] = o * lax.rsqrt(ms + RMS_EPS) * subln_ref[...] * (1.0 - LAMBDA_INIT)


def _suffix_matrix(page):
    flat = page * N_HEADS
    w = np.zeros((page, flat + LANES), np.float32)
    pos = np.arange(flat) // N_HEADS
    w[:, :flat] = (np.arange(page)[:, None] > pos[None, :]).astype(np.float32)
    w[:, flat:] = 1.0
    return jnp.asarray(w, BF16)


def _attend_decode(qd, qf, knd, vnd, knf, vnf, lf_new, caches, page_table, slopes, lam, subln, pages_per_step):
    cdk, cdv, cfk, cfv, clf = caches
    nb, n_pages = page_table.shape
    page = cdk.shape[1]
    g = pages_per_step
    wsuf = _suffix_matrix(page)
    nh = N_HEADS

    def page_spec(i, tail_shape):
        nd = len(tail_shape)
        return pl.BlockSpec((None, page) + tail_shape,
                            lambda b, j, pt: (pt[b * n_pages + n_pages - 1 - (j * g + i)],) + (0,) * (nd + 1))

    per_b = lambda shape: pl.BlockSpec((None,) + shape, lambda b, j, pt: (b,) + (0,) * len(shape))
    const = lambda a: pl.BlockSpec(a.shape, lambda b, j, pt: (0,) * a.ndim)
    in_specs = ([per_b((nh, HEAD_DIM))] * 6 + [per_b((nh, 1)), const(slopes), const(lam), const(subln), const(wsuf)]
                + [page_spec(i, (nh, HEAD_DIM)) for _ in range(4) for i in range(g)]
                + [page_spec(i, (nh,)) for i in range(g)])
    scratch = [pltpu.VMEM((2 * nh, 1), F32), pltpu.VMEM((2 * nh, 1), F32), pltpu.VMEM((2 * nh, HEAD_DIM), F32),
               pltpu.VMEM((nh, 1), F32), pltpu.VMEM((nh, 1), F32), pltpu.VMEM((nh, HEAD_DIM), F32),
               pltpu.VMEM((nh, LANES), F32)]
    grid_spec = pltpu.PrefetchScalarGridSpec(
        num_scalar_prefetch=1, grid=(nb, n_pages // g), in_specs=in_specs,
        out_specs=[per_b((nh, HEAD_DIM))] * 2, scratch_shapes=scratch)
    return pl.pallas_call(
        functools.partial(_decode_kernel, pages_per_step=g, n_pages=n_pages, page=page),
        grid_spec=grid_spec,
        out_shape=[jax.ShapeDtypeStruct((nb, nh, HEAD_DIM), F32)] * 2,
        compiler_params=pltpu.CompilerParams(dimension_semantics=("arbitrary", "arbitrary"),
                                             vmem_limit_bytes=VMEM_LIMIT),
        name="attn_decode",
    )(page_table.reshape(-1), qd, qf, knd, vnd, knf, vnf, lf_new, slopes, lam, subln, wsuf,
      *([cdk] * g + [cdv] * g + [cfk] * g + [cfv] * g + [clf] * g))


def _post_kernel(h_ref, o_ref, p_ref, wo_ref, g2_ref, wg_ref, wu_ref, wd_ref, gp_ref, wpg_ref, bpg_ref,
                 wpp_ref, gf_ref, y_ref, *, chunk):
    h = h_ref[...] + _dot(o_ref[...], wo_ref[...])
    hn = _rms(h, g2_ref[...]).astype(BF16)
    h = h + 0.5 * _swiglu(hn, wg_ref, wu_ref, wd_ref, chunk)
    hn = _rms(h, gp_ref[...]).astype(BF16)
    gate = jax.nn.sigmoid(_dot(hn, wpg_ref[...]) + bpg_ref[...])
    h = h + gate * _dot(p_ref[...].astype(BF16), wpp_ref[...])
    y_ref[...] = _rms(h, gf_ref[...])


def _post(h, o, p, wo, g2, wg, wu, wd, gp, wpg, bpg, wpp, gf, tm):
    n, d = h.shape
    row = lambda width: pl.BlockSpec((tm, width), lambda i: (i, 0))
    consts = (wo, g2, wg, wu, wd, gp, wpg, bpg, wpp, gf)
    return pl.pallas_call(
        functools.partial(_post_kernel, chunk=_ff_chunk(wg.shape[1])),
        grid=(n // tm,),
        in_specs=[row(d), row(o.shape[1]), row(p.shape[1])] + [_const_spec(c.shape) for c in consts],
        out_specs=row(d),
        out_shape=jax.ShapeDtypeStruct((n, d), F32),
        compiler_params=pltpu.CompilerParams(dimension_semantics=("arbitrary",), vmem_limit_bytes=VMEM_LIMIT),
        name="post",
    )(h, o, p, *consts)


def _row_tile(n, target):
    tm = min(n, target)
    assert n % tm == 0 and (tm % 8 == 0 or tm == n), (n, tm)
    return tm


def kernel(x_prompt, x_sample, cache_diff_k, cache_diff_v, cache_fox_k, cache_fox_v, cache_fox_logf, page_table, p_prompt, p_sample, ffn1_norm, ffn1_w_gate, ffn1_w_up, ffn1_w_down, mix_norm, w_in, b_forget, lambda_q1, lambda_k1, lambda_q2, lambda_k2, diff_subln, w_out, ffn2_norm, ffn2_w_gate, ffn2_w_up, ffn2_w_down, ple_norm, w_ple_gate, b_ple_gate, w_ple_proj, final_norm):
    depth = w_in.shape[0]
    assert depth == 1, "single-layer step only"
    d = x_prompt.shape[-1]
    w = HEADS_W
    assert w_in.shape[-1] == 6 * w + N_HEADS and w_out.shape[1:] == (2 * w, d)
    bf = lambda a: a.astype(BF16)
    row = lambda a: a.reshape(1, -1)

    wg1, wu1, wd1 = bf(ffn1_w_gate[0]), bf(ffn1_w_up[0]), bf(ffn1_w_down[0])
    wg2, wu2, wd2 = bf(ffn2_w_gate[0]), bf(ffn2_w_up[0]), bf(ffn2_w_down[0])
    w_qkv = bf(w_in[0, :, :6 * w])
    w_f = jnp.pad(bf(w_in[0, :, 6 * w:]), ((0, 0), (0, LANES - N_HEADS)))
    b_f = jnp.pad(b_forget[0], (0, LANES - N_HEADS)).reshape(1, LANES)
    wo, wpg, wpp = bf(w_out[0]), bf(w_ple_gate[0]), bf(w_ple_proj[0])
    lam = jnp.stack([lambda_q1[0], lambda_k1[0], lambda_q2[0], lambda_k2[0]])
    subln_pair = jnp.tile(diff_subln[0], 2).reshape(1, LANES)
    slopes = jnp.asarray(_alibi_slopes())

    def dense_pre(x3, tm):
        b, t, _ = x3.shape
        h1 = _ffn(x3.reshape(b * t, d), row(ffn1_norm[0]), wg1, wu1, wd1, tm)
        outs = _project(h1.reshape(b, t, d), row(mix_norm[0]), w_qkv, w_f, b_f, tm)
        return h1, outs

    def dense_post(h1, o, p2, tm):
        return _post(h1, o, p2, wo, row(ffn2_norm[0]), wg2, wu2, wd2, row(ple_norm[0]), wpg,
                     row(b_ple_gate[0]), wpp, row(final_norm), tm)

    def heads(a, b, t):
        return a.reshape(1, b, t, N_HEADS, HEAD_DIM)

    b, t, _ = x_prompt.shape
    tm = _row_tile(t, 512)
    h1, (q, k, v, kd, vd, kf, vf, lf, qaug, kaug) = dense_pre(x_prompt, tm)
    tq = _row_tile(t, 512)
    q_tab, k_tab = _alibi_tables(t, tq)
    o_d = _attend_prompt(q, k, v, q_tab[None], k_tab[None], slopes, lam, subln_pair, is_diff=True, tq=tq, tk=tq)
    o_f = _attend_prompt(q, k, v, qaug, kaug, slopes, lam, subln_pair, is_diff=False, tq=tq, tk=tq)
    o = jnp.concatenate([o_d, o_f], axis=-1).reshape(b * t, 2 * w)
    y_prompt = dense_post(h1, o, p_prompt[0].reshape(b * t, -1), tm).reshape(b, t, d)
    prompt_new = (heads(kd, b, t), heads(vd, b, t), heads(kf, b, t), heads(vf, b, t), lf.reshape(1, b, t, N_HEADS))

    nb, ts, _ = x_sample.shape
    assert ts == 1, "decode kernel handles one query per row"
    xs = x_sample.reshape(1, nb, d)
    h1s, (qs, _, _, kds, vds, kfs, vfs, lfs, _, _) = dense_pre(xs, nb)
    hd = lambda a: a.reshape(nb, N_HEADS, HEAD_DIM)
    caches = (cache_diff_k[0], cache_diff_v[0], cache_fox_k[0], cache_fox_v[0], cache_fox_logf[0])
    o_ds, o_fs = _attend_decode(hd(qs[0, :, :w]), hd(qs[0, :, w:]), hd(kds), hd(vds), hd(kfs), hd(vfs),
                                lfs.reshape(nb, N_HEADS, 1), caches, page_table, slopes.reshape(N_HEADS, 1),
                                lam, row(diff_subln[0]), pages_per_step=4)
    o_s = bf(jnp.concatenate([o_ds.reshape(nb, w), o_fs.reshape(nb, w)], axis=-1))
    y_sample = dense_post(h1s, o_s, p_sample[0].reshape(nb, -1), nb).reshape(nb, 1, d)
    sample_new = (heads(kds, nb, 1), heads(vds, nb, 1), heads(kfs, nb, 1), heads(vfs, nb, 1),
                  lfs.reshape(1, nb, 1, N_HEADS))

    return (y_prompt, y_sample) + prompt_new + sample_new
```

```python
import functools
import math

import numpy as np
import jax
import jax.numpy as jnp
from jax import lax
from jax.experimental import pallas as pl
from jax.experimental.pallas import tpu as pltpu

F32 = jnp.float32
BF16 = jnp.bfloat16

HEAD_DIM = 64
N_HEADS = 8
HEADS_W = N_HEADS * HEAD_DIM
DIFF_QK = HEAD_DIM // 2
LANES = 128
N_PAIRS = HEADS_W // LANES
AUG_PER_HEAD = 6
RMS_EPS = 1e-6
LOG2E = 1.4426950408889634
LAMBDA_INIT = 0.8 - 0.6 * math.exp(-0.3 * 0)
NEG_INF = float("-inf")
VMEM_LIMIT = 56 * 1024 * 1024
DECODE_PAGES_PER_STEP = 8


def _rms(x, g):
    return x * lax.rsqrt(jnp.mean(x * x, axis=-1, keepdims=True) + RMS_EPS) * g


def _dot(a, b):
    return jnp.dot(a, b, preferred_element_type=F32)


def _dot_nt(a, b):
    return lax.dot_general(a, b, (((1,), (1,)), ((), ())), preferred_element_type=F32)


def _split3(x):
    hi = x.astype(BF16).astype(F32)
    r = x - hi
    mid = r.astype(BF16).astype(F32)
    lo = (r - mid).astype(BF16).astype(F32)
    return hi, mid, lo


def _const_spec(shape):
    nd = len(shape)
    return pl.BlockSpec(shape, lambda *_: (0,) * nd, pipeline_mode=pl.Buffered(1))


def _swiglu(hn, wg_ref, wu_ref, wd_ref, chunk):
    d_ff = wg_ref.shape[1]
    out = None
    for c0 in range(0, d_ff, chunk):
        g = _dot(hn, wg_ref[:, c0:c0 + chunk])
        u = _dot(hn, wu_ref[:, c0:c0 + chunk])
        a = (g * jax.nn.sigmoid(g) * u).astype(BF16)
        part = _dot(a, wd_ref[c0:c0 + chunk, :])
        out = part if out is None else out + part
    return out


def _ff_chunk(d_ff):
    return d_ff


def _ffn_kernel(x_ref, g_ref, wg_ref, wu_ref, wd_ref, h_ref, *, chunk):
    x = x_ref[...]
    hn = _rms(x, g_ref[...]).astype(BF16)
    h_ref[...] = x + 0.5 * _swiglu(hn, wg_ref, wu_ref, wd_ref, chunk)


def _ffn(x, g, wg, wu, wd, tm):
    n, d = x.shape
    d_ff = wg.shape[1]
    return pl.pallas_call(
        functools.partial(_ffn_kernel, chunk=_ff_chunk(d_ff)),
        grid=(n // tm,),
        in_specs=[pl.BlockSpec((tm, d), lambda i: (i, 0)),
                  _const_spec((1, d)), _const_spec((d, d_ff)), _const_spec((d, d_ff)), _const_spec((d_ff, d))],
        out_specs=pl.BlockSpec((tm, d), lambda i: (i, 0)),
        out_shape=jax.ShapeDtypeStruct((n, d), F32),
        compiler_params=pltpu.CompilerParams(dimension_semantics=("arbitrary",), vmem_limit_bytes=VMEM_LIMIT),
        name="ffn1",
    )(x, g, wg, wu, wd)


def _proj_kernel(h_ref, g_ref, win_ref, wf_ref, bf_ref, tri_ref, selq_ref, selk_ref,
                 q_ref, k_ref, v_ref, kd_ref, vd_ref, kf_ref, vf_ref, lf_ref, qaug_ref, kaug_ref,
                 carry_ref, *, q_scale_d, q_scale_f):
    t = pl.program_id(1)
    w = HEADS_W

    @pl.when(t == 0)
    def _():
        carry_ref[...] = jnp.zeros_like(carry_ref)

    hn = _rms(h_ref[0], g_ref[...]).astype(BF16)
    z = _dot(hn, win_ref[...])
    kd, vd = z[:, w:2 * w], z[:, 2 * w:3 * w]
    kf, vf = z[:, 4 * w:5 * w], z[:, 5 * w:6 * w]
    kd_ref[0], vd_ref[0], kf_ref[0], vf_ref[0] = kd, vd, kf, vf
    q_ref[0, :, :w] = (z[:, :w] * q_scale_d).astype(BF16)
    q_ref[0, :, w:] = (z[:, 3 * w:4 * w] * q_scale_f).astype(BF16)
    k_ref[0, :, :w] = kd.astype(BF16)
    k_ref[0, :, w:] = kf.astype(BF16)
    v_ref[0, :, :w] = vd.astype(BF16)
    v_ref[0, :, w:] = vf.astype(BF16)

    x = _dot(hn, wf_ref[...]) + bf_ref[...]
    lane = lax.broadcasted_iota(jnp.int32, x.shape, 1)
    lf = -(jnp.maximum(-x, 0.0) + jnp.log1p(jnp.exp(-jnp.abs(x))))
    lf = jnp.where(lane < N_HEADS, lf, 0.0)
    lf_ref[0] = lf[:, :N_HEADS]

    hi, mid, lo = _split3(lf * LOG2E)
    tri = tri_ref[...]
    c = (_dot(tri, hi.astype(BF16)) + _dot(tri, mid.astype(BF16)) + _dot(tri, lo.astype(BF16))
         + carry_ref[...])
    tm = c.shape[0]
    carry_ref[...] = c[tm - 1:tm, :]

    chi, cmid, clo = _split3(c)
    packed = (chi + pltpu.roll(cmid, N_HEADS, 1) + pltpu.roll(clo, 2 * N_HEADS, 1)
              + jnp.where(lane == 3 * N_HEADS, 1.0, 0.0)).astype(BF16)
    qaug_ref[0] = _dot(packed, selq_ref[...]).astype(BF16)
    kaug_ref[0] = _dot(packed, selk_ref[...]).astype(BF16)


def _aug_select_matrices():
    selq = np.zeros((LANES, N_PAIRS * LANES), np.float32)
    selk = np.zeros((LANES, N_PAIRS * LANES), np.float32)
    one = 3 * N_HEADS
    for p in range(N_PAIRS):
        for s in range(2):
            head = 2 * p + s
            base = p * LANES + s * AUG_PER_HEAD
            for piece in range(3):
                selq[piece * N_HEADS + head, base + piece] = 1.0
                selq[one, base + 3 + piece] = 1.0
                selk[one, base + piece] = 1.0
                selk[piece * N_HEADS + head, base + 3 + piece] = -1.0
    return jnp.asarray(selq, BF16), jnp.asarray(selk, BF16)


def _project(h, g, w_in, w_f, b_f, tm):
    b, t, d = h.shape
    w = HEADS_W
    selq, selk = _aug_select_matrices()
    tri = jnp.asarray(np.tril(np.ones((tm, tm), np.float32)), BF16)
    row = lambda width: pl.BlockSpec((1, tm, width), lambda i, j: (i, j, 0))
    sds = lambda width, dt: jax.ShapeDtypeStruct((b, t, width), dt)
    kern = functools.partial(_proj_kernel,
                             q_scale_d=DIFF_QK ** -0.5 * LOG2E, q_scale_f=HEAD_DIM ** -0.5 * LOG2E)
    return pl.pallas_call(
        kern,
        grid=(b, t // tm),
        in_specs=[row(d), _const_spec((1, d)), _const_spec(w_in.shape), _const_spec(w_f.shape),
                  _const_spec((1, LANES)), _const_spec((tm, tm)), _const_spec(selq.shape), _const_spec(selk.shape)],
        out_specs=[row(2 * w), row(2 * w), row(2 * w), row(w), row(w), row(w), row(w),
                   row(N_HEADS), row(N_PAIRS * LANES), row(N_PAIRS * LANES)],
        out_shape=[sds(2 * w, BF16), sds(2 * w, BF16), sds(2 * w, BF16),
                   sds(w, F32), sds(w, F32), sds(w, F32), sds(w, F32),
                   sds(N_HEADS, F32), sds(N_PAIRS * LANES, BF16), sds(N_PAIRS * LANES, BF16)],
        scratch_shapes=[pltpu.VMEM((1, LANES), F32)],
        compiler_params=pltpu.CompilerParams(dimension_semantics=("arbitrary", "arbitrary"),
                                             vmem_limit_bytes=VMEM_LIMIT),
        name="project",
    )(h, g, w_in, w_f, b_f, tri, selq, selk)


def _alibi_kernel(q_ref, k_ref):
    rows = q_ref.shape[0]
    pos = (pl.program_id(0) * rows + lax.broadcasted_iota(jnp.int32, (rows, LANES), 0)).astype(F32)
    lane = lax.broadcasted_iota(jnp.int32, (rows, LANES), 1)
    hi, mid, lo = _split3(pos * LOG2E)
    pieces = jnp.where(lane % 3 == 0, hi, jnp.where(lane % 3 == 1, mid, lo))
    q_ref[...] = jnp.where(lane < 3, -pieces, jnp.where(lane < 6, 1.0, 0.0)).astype(BF16)
    k_ref[...] = jnp.where(lane < 3, 1.0, jnp.where(lane < 6, pieces, 0.0)).astype(BF16)


def _alibi_tables(n, rows):
    return pl.pallas_call(
        _alibi_kernel,
        grid=(n // rows,),
        out_specs=[pl.BlockSpec((rows, LANES), lambda i: (i, 0))] * 2,
        out_shape=[jax.ShapeDtypeStruct((n, LANES), BF16)] * 2,
        name="alibi_tables",
    )()


def _alibi_slopes():
    slopes = 2.0 ** (-8.0 * np.arange(1, N_HEADS + 1, dtype=np.float64) / N_HEADS)
    assert np.all(np.log2(slopes) == np.round(np.log2(slopes))), "bias columns assume power-of-two slopes"
    return slopes.astype(np.float32)


def _lambda_full(lam_ref):
    l = lam_ref[...]
    s1 = jnp.sum(l[0:1] * l[1:2], axis=-1, keepdims=True)
    s2 = jnp.sum(l[2:3] * l[3:4], axis=-1, keepdims=True)
    return jnp.exp(s1) - jnp.exp(s2) + LAMBDA_INIT


def _head_rms(o, subln_row, lane):
    sq = o * o
    first = lane < HEAD_DIM
    sum_a = jnp.sum(jnp.where(first, sq, 0.0), axis=-1, keepdims=True)
    sum_b = jnp.sum(jnp.where(first, 0.0, sq), axis=-1, keepdims=True)
    ms = jnp.where(first, sum_a, sum_b) * (1.0 / HEAD_DIM)
    return o * lax.rsqrt(ms + RMS_EPS) * subln_row


def _attn_kernel(slope_ref, q_ref, k_ref, v_ref, qaug_ref, kaug_ref, lam_ref, subln_ref, o_ref,
                 qs_ref, m_ref, acc_ref, *, is_diff, tq, tk):
    n_maps = 4 if is_diff else 2
    rows = n_maps * tq
    pair = pl.program_id(1)
    qi = pl.program_id(2)
    lane = lax.broadcasted_iota(jnp.int32, (tq, LANES), 1)

    q = q_ref[0].astype(F32)
    qa = qaug_ref[0].astype(F32)
    map_w = LANES // n_maps
    for m in range(n_maps):
        qm = jnp.where((lane >= m * map_w) & (lane < (m + 1) * map_w), q, 0.0)
        if is_diff:
            aug = qa * slope_ref[2 * pair + m // 2]
        else:
            aug = jnp.where((lane >= m * AUG_PER_HEAD) & (lane < (m + 1) * AUG_PER_HEAD), qa, 0.0)
        qs_ref[m * tq:(m + 1) * tq, :LANES] = qm.astype(BF16)
        qs_ref[m * tq:(m + 1) * tq, LANES:] = aug.astype(BF16)

    m_ref[...] = jnp.full_like(m_ref, NEG_INF)
    acc_ref[...] = jnp.zeros_like(acc_ref)
    ones = jnp.ones((tk, LANES), BF16)

    def step(j, masked):
        start = pl.multiple_of(j * tk, tk)
        kfull = jnp.concatenate([k_ref[0, pl.ds(start, tk), :], kaug_ref[0, pl.ds(start, tk), :]], axis=1)
        s = _dot_nt(qs_ref[...], kfull)
        if masked:
            r = lax.broadcasted_iota(jnp.int32, (rows, tk), 0) % tq + qi * tq
            c = lax.broadcasted_iota(jnp.int32, (rows, tk), 1) + start
            s = jnp.where(c <= r, s, NEG_INF)
        m_prev = m_ref[...]
        m_new = jnp.maximum(m_prev, jnp.max(s, axis=1, keepdims=True))
        p = jnp.exp2(s - jnp.concatenate([m_new] * (tk // LANES), axis=1)).astype(BF16)
        alpha = jnp.exp2(m_prev - m_new)
        vfull = jnp.concatenate([v_ref[0, pl.ds(start, tk), :], ones], axis=1)
        acc_ref[...] = acc_ref[...] * jnp.concatenate([alpha, alpha], axis=1) + _dot(p, vfull)
        m_ref[...] = m_new

    n_full = (qi * tq) // tk

    def body(j, carry):
        step(j, False)
        return carry

    lax.fori_loop(0, n_full, body, 0)
    for d in range(tq // tk):
        step(n_full + d, True)

    def normalized(m):
        a = acc_ref[m * tq:(m + 1) * tq, :]
        return a[:, :LANES] / a[:, LANES:]

    first = lane < HEAD_DIM
    if is_diff:
        lam = _lambda_full(lam_ref)
        o = jnp.where(first, normalized(0) - lam * normalized(1), normalized(2) - lam * normalized(3))
        o = _head_rms(o, subln_ref[...], lane) * (1.0 - LAMBDA_INIT)
    else:
        o = jnp.where(first, normalized(0), normalized(1))
    o_ref[0] = o.astype(o_ref.dtype)


def _attend_prompt(q, k, v, qaug, kaug, slopes, lam, subln, *, is_diff, tq, tk):
    b, t, _ = q.shape
    n_maps = 4 if is_diff else 2
    col0 = 0 if is_diff else N_PAIRS
    rows = n_maps * tq
    if is_diff:
        qaug_spec = pl.BlockSpec((1, tq, LANES), lambda i, p, j, s: (0, j, 0))
        kaug_spec = pl.BlockSpec((1, t, LANES), lambda i, p, j, s: (0, 0, 0))
    else:
        qaug_spec = pl.BlockSpec((1, tq, LANES), lambda i, p, j, s: (i, j, p))
        kaug_spec = pl.BlockSpec((1, t, LANES), lambda i, p, j, s: (i, 0, p))
    grid_spec = pltpu.PrefetchScalarGridSpec(
        num_scalar_prefetch=1,
        grid=(b, N_PAIRS, t // tq),
        in_specs=[pl.BlockSpec((1, tq, LANES), lambda i, p, j, s: (i, j, col0 + p)),
                  pl.BlockSpec((1, t, LANES), lambda i, p, j, s: (i, 0, col0 + p)),
                  pl.BlockSpec((1, t, LANES), lambda i, p, j, s: (i, 0, col0 + p)),
                  qaug_spec, kaug_spec,
                  pl.BlockSpec(lam.shape, lambda i, p, j, s: (0, 0)),
                  pl.BlockSpec(subln.shape, lambda i, p, j, s: (0, 0))],
        out_specs=pl.BlockSpec((1, tq, LANES), lambda i, p, j, s: (i, j, p)),
        scratch_shapes=[pltpu.VMEM((rows, 2 * LANES), BF16),
                        pltpu.VMEM((rows, LANES), F32),
                        pltpu.VMEM((rows, 2 * LANES), F32)],
    )
    return pl.pallas_call(
        functools.partial(_attn_kernel, is_diff=is_diff, tq=tq, tk=tk),
        grid_spec=grid_spec,
        out_shape=jax.ShapeDtypeStruct((b, t, HEADS_W), BF16),
        compiler_params=pltpu.CompilerParams(dimension_semantics=("arbitrary",) * 3,
                                             vmem_limit_bytes=VMEM_LIMIT),
        name="attn_diff" if is_diff else "attn_fox",
    )(slopes, q, k, v, qaug, kaug, lam, subln)


def _decode_kernel(pt_ref, qd_ref, qf_ref, knd_ref, vnd_ref, knf_ref, vnf_ref, lfn_ref, slope_ref,
                   lam_ref, subln_ref, usuf_ref, *rest, pages_per_step, n_pages, page):
    g = pages_per_step
    dk_refs, dv_refs = rest[0:g], rest[g:2 * g]
    fk_refs, fv_refs, lf_refs = rest[2 * g:3 * g], rest[3 * g:4 * g], rest[4 * g:5 * g]
    od_ref, of_ref = rest[5 * g], rest[5 * g + 1]
    qbd_ref, qbf_ref, md_ref, ld_ref, accd_ref, mf_ref, lsum_ref, accf_ref, tail_ref = rest[5 * g + 2:]
    del pt_ref
    j = pl.program_id(1)
    nh = N_HEADS
    w = HEADS_W
    sub = lax.broadcasted_iota(jnp.int32, (2 * nh, w), 0)
    lane = lax.broadcasted_iota(jnp.int32, (2 * nh, w), 1)
    own = (lane // HEAD_DIM) == (sub % nh)
    own8 = own[:nh]

    @pl.when(j == 0)
    def _():
        half = ((lane % HEAD_DIM) < DIFF_QK) == (sub < nh)
        qbd = jnp.where(own & half, jnp.broadcast_to(qd_ref[...].astype(F32), (2 * nh, w)), 0.0)
        qbf = jnp.where(own8, jnp.broadcast_to(qf_ref[...].astype(F32), (nh, w)), 0.0)
        qbd_ref[...] = qbd.astype(BF16)
        qbf_ref[...] = qbf.astype(BF16)
        md_ref[...] = jnp.sum(qbd * knd_ref[...], axis=-1, keepdims=True)
        ld_ref[...] = jnp.ones_like(ld_ref)
        accd_ref[...] = jnp.broadcast_to(vnd_ref[...], (2 * nh, w))
        mf_ref[...] = jnp.sum(qbf * knf_ref[...], axis=-1, keepdims=True)
        lsum_ref[...] = jnp.ones_like(lsum_ref)
        accf_ref[...] = jnp.broadcast_to(vnf_ref[...], (nh, w))
        tail_ref[...] = jnp.zeros_like(tail_ref)

    past = n_pages * page
    pos_in_page = lax.broadcasted_iota(jnp.int32, (2 * nh, page), 1)
    slope2 = jnp.concatenate([slope_ref[...], slope_ref[...]], axis=0) * LOG2E
    lf_new = lfn_ref[...] * LOG2E
    usuf = usuf_ref[...]
    tail = tail_ref[...]
    sd, sf = [], []
    for i in range(g):
        logical = n_pages - 1 - (j * g + i)
        s = _dot(qbd_ref[...], dk_refs[i][...].astype(BF16))
        dist = (past - (logical * page + pos_in_page)).astype(F32)
        sd.append(s - slope2 * dist)
        s = _dot(qbf_ref[...], fk_refs[i][...].astype(BF16))
        hi, mid, lo = _split3(lf_refs[i][...] * LOG2E)
        suf = _dot(hi.astype(BF16), usuf) + _dot(mid.astype(BF16), usuf) + _dot(lo.astype(BF16), usuf)
        sf.append(s + suf[:, :page] + tail + lf_new)
        tail = tail + suf[:, page:]
    tail_ref[...] = tail

    def online(parts, m_ref, l_ref, acc_ref, v_refs):
        s = jnp.concatenate(parts, axis=1)
        m_prev = m_ref[...]
        m_new = jnp.maximum(m_prev, jnp.max(s, axis=-1, keepdims=True))
        p = jnp.exp2(s - m_new)
        alpha = jnp.exp2(m_prev - m_new)
        l_ref[...] = alpha * l_ref[...] + jnp.sum(p, axis=-1, keepdims=True)
        pv = None
        for i in range(g):
            part = _dot_nt(p[:, i * page:(i + 1) * page].astype(BF16), v_refs[i][...].astype(BF16))
            pv = part if pv is None else pv + part
        acc_ref[...] = alpha * acc_ref[...] + pv
        m_ref[...] = m_new

    online(sd, md_ref, ld_ref, accd_ref, dv_refs)
    online(sf, mf_ref, lsum_ref, accf_ref, fv_refs)

    @pl.when(j == pl.num_programs(1) - 1)
    def _():
        o_f = jnp.where(own8, accf_ref[...] / lsum_ref[...], 0.0)
        of_ref[...] = jnp.sum(o_f, axis=0, keepdims=True)
        lam = _lambda_full(lam_ref)
        a = accd_ref[...] / ld_ref[...]
        o = jnp.where(own8, a[:nh] - lam * a[nh:], 0.0)
        ms = jnp.sum(o * o, axis=-1, keepdims=True) * (1.0 / HEAD_DIM)
        o = o * lax.rsqrt(ms + RMS_EPS)
        od_ref[...] = jnp.sum(o, axis=0, keepdims=True) * subln_ref[...] * (1.0 - LAMBDA_INIT)


def _suffix_matrix(page):
    u = np.ones((page, 2 * page), np.float32)
    u[:, :page] = np.arange(page)[:, None] > np.arange(page)[None, :]
    return jnp.asarray(u, BF16)


def _attend_decode(qd, qf, knd, vnd, knf, vnf, lf_new, caches, page_table, slopes, lam, subln, pages_per_step):
    cdk, cdv, cfk, cfv, clf = caches
    nb, n_pages = page_table.shape
    page = cdk.shape[-1]
    assert page == LANES
    g = pages_per_step
    usuf = _suffix_matrix(page)
    nh = N_HEADS
    w = HEADS_W

    def page_spec(i, rows):
        return pl.BlockSpec((None, rows, page),
                            lambda b, j, pt: (pt[b * n_pages + n_pages - 1 - (j * g + i)], 0, 0))

    per_b = lambda shape: pl.BlockSpec((None,) + shape, lambda b, j, pt: (b,) + (0,) * len(shape))
    const = lambda a: pl.BlockSpec(a.shape, lambda b, j, pt: (0,) * a.ndim)
    in_specs = ([per_b((1, w))] * 6 + [per_b((nh, 1)), const(slopes), const(lam), const(subln), const(usuf)]
                + [page_spec(i, w) for _ in range(4) for i in range(g)]
                + [page_spec(i, nh) for i in range(g)])
    scratch = [pltpu.VMEM((2 * nh, w), BF16), pltpu.VMEM((nh, w), BF16),
               pltpu.VMEM((2 * nh, 1), F32), pltpu.VMEM((2 * nh, 1), F32), pltpu.VMEM((2 * nh, w), F32),
               pltpu.VMEM((nh, 1), F32), pltpu.VMEM((nh, 1), F32), pltpu.VMEM((nh, w), F32),
               pltpu.VMEM((nh, page), F32)]
    grid_spec = pltpu.PrefetchScalarGridSpec(
        num_scalar_prefetch=1, grid=(nb, n_pages // g), in_specs=in_specs,
        out_specs=[per_b((1, w))] * 2, scratch_shapes=scratch)
    return pl.pallas_call(
        functools.partial(_decode_kernel, pages_per_step=g, n_pages=n_pages, page=page),
        grid_spec=grid_spec,
        out_shape=[jax.ShapeDtypeStruct((nb, 1, w), F32)] * 2,
        compiler_params=pltpu.CompilerParams(dimension_semantics=("arbitrary", "arbitrary"),
                                             vmem_limit_bytes=VMEM_LIMIT),
        name="attn_decode",
    )(page_table.reshape(-1), qd, qf, knd, vnd, knf, vnf, lf_new, slopes, lam, subln, usuf,
      *([cdk] * g + [cdv] * g + [cfk] * g + [cfv] * g + [clf] * g))


def _post_kernel(h_ref, o_ref, p_ref, wo_ref, g2_ref, wg_ref, wu_ref, wd_ref, gp_ref, wpg_ref, bpg_ref,
                 wpp_ref, gf_ref, y_ref, *, chunk):
    h = h_ref[...] + _dot(o_ref[...], wo_ref[...])
    hn = _rms(h, g2_ref[...]).astype(BF16)
    h = h + 0.5 * _swiglu(hn, wg_ref, wu_ref, wd_ref, chunk)
    hn = _rms(h, gp_ref[...]).astype(BF16)
    gate = jax.nn.sigmoid(_dot(hn, wpg_ref[...]) + bpg_ref[...])
    h = h + gate * _dot(p_ref[...].astype(BF16), wpp_ref[...])
    y_ref[...] = _rms(h, gf_ref[...])


def _post(h, o, p, wo, g2, wg, wu, wd, gp, wpg, bpg, wpp, gf, tm):
    n, d = h.shape
    row = lambda width: pl.BlockSpec((tm, width), lambda i: (i, 0))
    consts = (wo, g2, wg, wu, wd, gp, wpg, bpg, wpp, gf)
    return pl.pallas_call(
        functools.partial(_post_kernel, chunk=_ff_chunk(wg.shape[1])),
        grid=(n // tm,),
        in_specs=[row(d), row(o.shape[1]), row(p.shape[1])] + [_const_spec(c.shape) for c in consts],
        out_specs=row(d),
        out_shape=jax.ShapeDtypeStruct((n, d), F32),
        compiler_params=pltpu.CompilerParams(dimension_semantics=("arbitrary",), vmem_limit_bytes=VMEM_LIMIT),
        name="post",
    )(h, o, p, *consts)


def _row_tile(n, target):
    tm = min(n, target)
    assert n % tm == 0 and (tm % 8 == 0 or tm == n), (n, tm)
    return tm


def kernel(x_prompt, x_sample, cache_diff_k, cache_diff_v, cache_fox_k, cache_fox_v, cache_fox_logf, page_table, p_prompt, p_sample, ffn1_norm, ffn1_w_gate, ffn1_w_up, ffn1_w_down, mix_norm, w_in, b_forget, lambda_q1, lambda_k1, lambda_q2, lambda_k2, diff_subln, w_out, ffn2_norm, ffn2_w_gate, ffn2_w_up, ffn2_w_down, ple_norm, w_ple_gate, b_ple_gate, w_ple_proj, final_norm):
    depth = w_in.shape[0]
    assert depth == 1, "single-layer step only"
    d = x_prompt.shape[-1]
    w = HEADS_W
    assert w_in.shape[-1] == 6 * w + N_HEADS and w_out.shape[1:] == (2 * w, d)
    bf = lambda a: a.astype(BF16)
    row = lambda a: a.reshape(1, -1)

    wg1, wu1, wd1 = bf(ffn1_w_gate[0]), bf(ffn1_w_up[0]), bf(ffn1_w_down[0])
    wg2, wu2, wd2 = bf(ffn2_w_gate[0]), bf(ffn2_w_up[0]), bf(ffn2_w_down[0])
    w_qkv = bf(w_in[0, :, :6 * w])
    w_f = jnp.pad(bf(w_in[0, :, 6 * w:]), ((0, 0), (0, LANES - N_HEADS)))
    b_f = jnp.pad(b_forget[0], (0, LANES - N_HEADS)).reshape(1, LANES)
    wo, wpg, wpp = bf(w_out[0]), bf(w_ple_gate[0]), bf(w_ple_proj[0])
    lam = jnp.stack([lambda_q1[0], lambda_k1[0], lambda_q2[0], lambda_k2[0]])
    subln_pair = jnp.tile(diff_subln[0], 2).reshape(1, LANES)
    slopes = jnp.asarray(_alibi_slopes())

    def dense_pre(x3, tm):
        b, t, _ = x3.shape
        h1 = _ffn(x3.reshape(b * t, d), row(ffn1_norm[0]), wg1, wu1, wd1, tm)
        outs = _project(h1.reshape(b, t, d), row(mix_norm[0]), w_qkv, w_f, b_f, tm)
        return h1, outs

    def dense_post(h1, o, p2, tm):
        return _post(h1, o, p2, wo, row(ffn2_norm[0]), wg2, wu2, wd2, row(ple_norm[0]), wpg,
                     row(b_ple_gate[0]), wpp, row(final_norm), tm)

    def heads(a, b, t):
        return a.reshape(1, b, t, N_HEADS, HEAD_DIM)

    b, t, _ = x_prompt.shape
    tm = _row_tile(t, 512)
    h1, (q, k, v, kd, vd, kf, vf, lf, qaug, kaug) = dense_pre(x_prompt, tm)
    tq = _row_tile(t, 512)
    q_tab, k_tab = _alibi_tables(t, tq)
    o_d = _attend_prompt(q, k, v, q_tab[None], k_tab[None], slopes, lam, subln_pair, is_diff=True, tq=tq, tk=tq)
    o_f = _attend_prompt(q, k, v, qaug, kaug, slopes, lam, subln_pair, is_diff=False, tq=tq, tk=tq)
    o = jnp.concatenate([o_d, o_f], axis=-1).reshape(b * t, 2 * w)
    y_prompt = dense_post(h1, o, p_prompt[0].reshape(b * t, -1), tm).reshape(b, t, d)
    prompt_new = (heads(kd, b, t), heads(vd, b, t), heads(kf, b, t), heads(vf, b, t), lf.reshape(1, b, t, N_HEADS))

    nb, ts, _ = x_sample.shape
    assert ts == 1, "decode kernel handles one query per row"
    xs = x_sample.reshape(1, nb, d)
    h1s, (qs, _, _, kds, vds, kfs, vfs, lfs, _, _) = dense_pre(xs, nb)
    flat = lambda a: a.reshape(nb, 1, w)
    kv_view = lambda c: jnp.transpose(c[0], (0, 2, 3, 1)).reshape(c.shape[1], w, c.shape[2])
    caches = (kv_view(cache_diff_k), kv_view(cache_diff_v), kv_view(cache_fox_k), kv_view(cache_fox_v),
              jnp.transpose(cache_fox_logf[0], (0, 2, 1)))
    o_ds, o_fs = _attend_decode(flat(qs[0, :, :w]), flat(qs[0, :, w:]), flat(kds), flat(vds), flat(kfs), flat(vfs),
                                lfs.reshape(nb, N_HEADS, 1), caches, page_table, slopes.reshape(N_HEADS, 1),
                                lam, jnp.tile(diff_subln[0], N_HEADS).reshape(1, w),
                                pages_per_step=DECODE_PAGES_PER_STEP)
    o_s = bf(jnp.concatenate([o_ds.reshape(nb, w), o_fs.reshape(nb, w)], axis=-1))
    y_sample = dense_post(h1s, o_s, p_sample[0].reshape(nb, -1), nb).reshape(nb, 1, d)
    sample_new = (heads(kds, nb, 1), heads(vds, nb, 1), heads(kfs, nb, 1), heads(vfs, nb, 1),
                  lfs.reshape(1, nb, 1, N_HEADS))

    return (y_prompt, y_sample) + prompt_new + sample_new
```

```python
import functools
import math

import numpy as np
import jax
import jax.numpy as jnp
from jax import lax
from jax.experimental import pallas as pl
from jax.experimental.pallas import tpu as pltpu

F32 = jnp.float32
BF16 = jnp.bfloat16

HEAD_DIM = 64
N_HEADS = 8
HEADS_W = N_HEADS * HEAD_DIM
DIFF_QK = HEAD_DIM // 2
LANES = 128
N_PAIRS = HEADS_W // LANES
AUG_PER_HEAD = 6
RMS_EPS = 1e-6
LOG2E = 1.4426950408889634
LAMBDA_INIT = 0.8 - 0.6 * math.exp(-0.3 * 0)
NEG_INF = float("-inf")
VMEM_LIMIT = 56 * 1024 * 1024
DECODE_PAGES_PER_STEP = 8
ATTN_ROW_GROUP = 256


def _rms(x, g):
    return x * lax.rsqrt(jnp.mean(x * x, axis=-1, keepdims=True) + RMS_EPS) * g


def _dot(a, b):
    return jnp.dot(a, b, preferred_element_type=F32)


def _dot_nt(a, b):
    return lax.dot_general(a, b, (((1,), (1,)), ((), ())), preferred_element_type=F32)


def _split3(x):
    hi = x.astype(BF16).astype(F32)
    r = x - hi
    mid = r.astype(BF16).astype(F32)
    lo = (r - mid).astype(BF16).astype(F32)
    return hi, mid, lo


def _const_spec(shape):
    nd = len(shape)
    return pl.BlockSpec(shape, lambda *_: (0,) * nd, pipeline_mode=pl.Buffered(1))


def _swiglu(hn, wg_ref, wu_ref, wd_ref, chunk):
    d_ff = wg_ref.shape[1]
    out = None
    for c0 in range(0, d_ff, chunk):
        g = _dot(hn, wg_ref[:, c0:c0 + chunk])
        u = _dot(hn, wu_ref[:, c0:c0 + chunk])
        a = (g * jax.nn.sigmoid(g) * u).astype(BF16)
        part = _dot(a, wd_ref[c0:c0 + chunk, :])
        out = part if out is None else out + part
    return out


def _ff_chunk(d_ff):
    return d_ff


def _ffn_kernel(x_ref, g_ref, wg_ref, wu_ref, wd_ref, h_ref, *, chunk):
    x = x_ref[...]
    hn = _rms(x, g_ref[...]).astype(BF16)
    h_ref[...] = x + 0.5 * _swiglu(hn, wg_ref, wu_ref, wd_ref, chunk)


def _ffn(x, g, wg, wu, wd, tm):
    n, d = x.shape
    d_ff = wg.shape[1]
    return pl.pallas_call(
        functools.partial(_ffn_kernel, chunk=_ff_chunk(d_ff)),
        grid=(n // tm,),
        in_specs=[pl.BlockSpec((tm, d), lambda i: (i, 0)),
                  _const_spec((1, d)), _const_spec((d, d_ff)), _const_spec((d, d_ff)), _const_spec((d_ff, d))],
        out_specs=pl.BlockSpec((tm, d), lambda i: (i, 0)),
        out_shape=jax.ShapeDtypeStruct((n, d), F32),
        compiler_params=pltpu.CompilerParams(dimension_semantics=("arbitrary",), vmem_limit_bytes=VMEM_LIMIT),
        name="ffn1",
    )(x, g, wg, wu, wd)


def _proj_kernel(h_ref, g_ref, win_ref, wf_ref, bf_ref, tri_ref, selq_ref, selk_ref,
                 q_ref, k_ref, v_ref, kd_ref, vd_ref, kf_ref, vf_ref, lf_ref, qaug_ref, kaug_ref,
                 carry_ref, *, q_scale_d, q_scale_f):
    t = pl.program_id(1)
    w = HEADS_W

    @pl.when(t == 0)
    def _():
        carry_ref[...] = jnp.zeros_like(carry_ref)

    hn = _rms(h_ref[0], g_ref[...]).astype(BF16)
    z = _dot(hn, win_ref[...])
    kd, vd = z[:, w:2 * w], z[:, 2 * w:3 * w]
    kf, vf = z[:, 4 * w:5 * w], z[:, 5 * w:6 * w]
    kd_ref[0], vd_ref[0], kf_ref[0], vf_ref[0] = kd, vd, kf, vf
    q_ref[0, :, :w] = (z[:, :w] * q_scale_d).astype(BF16)
    q_ref[0, :, w:] = (z[:, 3 * w:4 * w] * q_scale_f).astype(BF16)
    k_ref[0, :, :w] = kd.astype(BF16)
    k_ref[0, :, w:] = kf.astype(BF16)
    v_ref[0, :, :w] = vd.astype(BF16)
    v_ref[0, :, w:] = vf.astype(BF16)

    x = _dot(hn, wf_ref[...]) + bf_ref[...]
    lane = lax.broadcasted_iota(jnp.int32, x.shape, 1)
    lf = -(jnp.maximum(-x, 0.0) + jnp.log1p(jnp.exp(-jnp.abs(x))))
    lf = jnp.where(lane < N_HEADS, lf, 0.0)
    lf_ref[0] = lf[:, :N_HEADS]

    hi, mid, lo = _split3(lf * LOG2E)
    tri = tri_ref[...]
    c = (_dot(tri, hi.astype(BF16)) + _dot(tri, mid.astype(BF16)) + _dot(tri, lo.astype(BF16))
         + carry_ref[...])
    tm = c.shape[0]
    carry_ref[...] = c[tm - 1:tm, :]

    chi, cmid, clo = _split3(c)
    packed = (chi + pltpu.roll(cmid, N_HEADS, 1) + pltpu.roll(clo, 2 * N_HEADS, 1)
              + jnp.where(lane == 3 * N_HEADS, 1.0, 0.0)).astype(BF16)
    qaug_ref[0] = _dot(packed, selq_ref[...]).astype(BF16)
    kaug_ref[0] = _dot(packed, selk_ref[...]).astype(BF16)


def _aug_select_matrices():
    selq = np.zeros((LANES, N_PAIRS * LANES), np.float32)
    selk = np.zeros((LANES, N_PAIRS * LANES), np.float32)
    one = 3 * N_HEADS
    for p in range(N_PAIRS):
        for s in range(2):
            head = 2 * p + s
            base = p * LANES + s * AUG_PER_HEAD
            for piece in range(3):
                selq[piece * N_HEADS + head, base + piece] = 1.0
                selq[one, base + 3 + piece] = 1.0
                selk[one, base + piece] = 1.0
                selk[piece * N_HEADS + head, base + 3 + piece] = -1.0
    return jnp.asarray(selq, BF16), jnp.asarray(selk, BF16)


def _project(h, g, w_in, w_f, b_f, tm):
    b, t, d = h.shape
    w = HEADS_W
    selq, selk = _aug_select_matrices()
    tri = jnp.asarray(np.tril(np.ones((tm, tm), np.float32)), BF16)
    row = lambda width: pl.BlockSpec((1, tm, width), lambda i, j: (i, j, 0))
    sds = lambda width, dt: jax.ShapeDtypeStruct((b, t, width), dt)
    kern = functools.partial(_proj_kernel,
                             q_scale_d=DIFF_QK ** -0.5 * LOG2E, q_scale_f=HEAD_DIM ** -0.5 * LOG2E)
    return pl.pallas_call(
        kern,
        grid=(b, t // tm),
        in_specs=[row(d), _const_spec((1, d)), _const_spec(w_in.shape), _const_spec(w_f.shape),
                  _const_spec((1, LANES)), _const_spec((tm, tm)), _const_spec(selq.shape), _const_spec(selk.shape)],
        out_specs=[row(2 * w), row(2 * w), row(2 * w), row(w), row(w), row(w), row(w),
                   row(N_HEADS), row(N_PAIRS * LANES), row(N_PAIRS * LANES)],
        out_shape=[sds(2 * w, BF16), sds(2 * w, BF16), sds(2 * w, BF16),
                   sds(w, F32), sds(w, F32), sds(w, F32), sds(w, F32),
                   sds(N_HEADS, F32), sds(N_PAIRS * LANES, BF16), sds(N_PAIRS * LANES, BF16)],
        scratch_shapes=[pltpu.VMEM((1, LANES), F32)],
        compiler_params=pltpu.CompilerParams(dimension_semantics=("arbitrary", "arbitrary"),
                                             vmem_limit_bytes=VMEM_LIMIT),
        name="project",
    )(h, g, w_in, w_f, b_f, tri, selq, selk)


def _alibi_kernel(q_ref, k_ref):
    rows = q_ref.shape[0]
    pos = (pl.program_id(0) * rows + lax.broadcasted_iota(jnp.int32, (rows, LANES), 0)).astype(F32)
    lane = lax.broadcasted_iota(jnp.int32, (rows, LANES), 1)
    hi, mid, lo = _split3(pos * LOG2E)
    pieces = jnp.where(lane % 3 == 0, hi, jnp.where(lane % 3 == 1, mid, lo))
    q_ref[...] = jnp.where(lane < 3, -pieces, jnp.where(lane < 6, 1.0, 0.0)).astype(BF16)
    k_ref[...] = jnp.where(lane < 3, 1.0, jnp.where(lane < 6, pieces, 0.0)).astype(BF16)


def _alibi_tables(n, rows):
    return pl.pallas_call(
        _alibi_kernel,
        grid=(n // rows,),
        out_specs=[pl.BlockSpec((rows, LANES), lambda i: (i, 0))] * 2,
        out_shape=[jax.ShapeDtypeStruct((n, LANES), BF16)] * 2,
        name="alibi_tables",
    )()


def _alibi_slopes():
    slopes = 2.0 ** (-8.0 * np.arange(1, N_HEADS + 1, dtype=np.float64) / N_HEADS)
    assert np.all(np.log2(slopes) == np.round(np.log2(slopes))), "bias columns assume power-of-two slopes"
    return slopes.astype(np.float32)


def _lambda_full(lam_ref):
    l = lam_ref[...]
    s1 = jnp.sum(l[0:1] * l[1:2], axis=-1, keepdims=True)
    s2 = jnp.sum(l[2:3] * l[3:4], axis=-1, keepdims=True)
    return jnp.exp(s1) - jnp.exp(s2) + LAMBDA_INIT


def _head_rms(o, subln_row, lane):
    sq = o * o
    first = lane < HEAD_DIM
    sum_a = jnp.sum(jnp.where(first, sq, 0.0), axis=-1, keepdims=True)
    sum_b = jnp.sum(jnp.where(first, 0.0, sq), axis=-1, keepdims=True)
    ms = jnp.where(first, sum_a, sum_b) * (1.0 / HEAD_DIM)
    return o * lax.rsqrt(ms + RMS_EPS) * subln_row


def _attn_kernel(slope_ref, q_ref, k_ref, v_ref, qaug_ref, kaug_ref, lam_ref, subln_ref, o_ref,
                 qs_ref, m_ref, acc_ref, *, is_diff, tq, tk, rg):
    n_maps = 4 if is_diff else 2
    rows = n_maps * tq
    pair = pl.program_id(1)
    qi = pl.program_id(2)
    lane = lax.broadcasted_iota(jnp.int32, (tq, LANES), 1)

    q = q_ref[0].astype(F32)
    qa = qaug_ref[0].astype(F32)
    map_w = LANES // n_maps
    for m in range(n_maps):
        qm = jnp.where((lane >= m * map_w) & (lane < (m + 1) * map_w), q, 0.0)
        if is_diff:
            aug = qa * slope_ref[2 * pair + m // 2]
        else:
            aug = jnp.where((lane >= m * AUG_PER_HEAD) & (lane < (m + 1) * AUG_PER_HEAD), qa, 0.0)
        qs_ref[m * tq:(m + 1) * tq, :LANES] = qm.astype(BF16)
        qs_ref[m * tq:(m + 1) * tq, LANES:] = aug.astype(BF16)

    m_ref[...] = jnp.full_like(m_ref, NEG_INF)
    acc_ref[...] = jnp.zeros_like(acc_ref)
    ones = jnp.ones((tk, LANES), BF16)

    def step(j, masked):
        start = pl.multiple_of(j * tk, tk)
        kfull = jnp.concatenate([k_ref[0, pl.ds(start, tk), :], kaug_ref[0, pl.ds(start, tk), :]], axis=1)
        vfull = jnp.concatenate([v_ref[0, pl.ds(start, tk), :], ones], axis=1)
        for r0 in range(0, rows, rg):
            rs = slice(r0, r0 + rg)
            s = _dot_nt(qs_ref[rs, :], kfull)
            if masked:
                r = lax.broadcasted_iota(jnp.int32, (rg, tk), 0) + (r0 % tq + qi * tq)
                c = lax.broadcasted_iota(jnp.int32, (rg, tk), 1) + start
                s = jnp.where(c <= r, s, NEG_INF)
            m_prev = m_ref[rs, :]
            m_new = jnp.maximum(m_prev, jnp.max(s, axis=1, keepdims=True))
            p = jnp.exp2(s - jnp.concatenate([m_new] * (tk // LANES), axis=1)).astype(BF16)
            alpha = jnp.exp2(m_prev - m_new)
            acc_ref[rs, :] = acc_ref[rs, :] * jnp.concatenate([alpha, alpha], axis=1) + _dot(p, vfull)
            m_ref[rs, :] = m_new

    n_full = (qi * tq) // tk

    def body(jj, carry):
        step(2 * jj, False)
        step(2 * jj + 1, False)
        return carry

    lax.fori_loop(0, n_full // 2, body, 0)

    @pl.when(n_full % 2 == 1)
    def _():
        step(n_full - 1, False)

    for d in range(tq // tk):
        step(n_full + d, True)

    def normalized(m):
        a = acc_ref[m * tq:(m + 1) * tq, :]
        return a[:, :LANES] / a[:, LANES:]

    first = lane < HEAD_DIM
    if is_diff:
        lam = _lambda_full(lam_ref)
        o = jnp.where(first, normalized(0) - lam * normalized(1), normalized(2) - lam * normalized(3))
        o = _head_rms(o, subln_ref[...], lane) * (1.0 - LAMBDA_INIT)
    else:
        o = jnp.where(first, normalized(0), normalized(1))
    o_ref[0] = o.astype(o_ref.dtype)


def _attend_prompt(q, k, v, qaug, kaug, slopes, lam, subln, *, is_diff, tq, tk, rg):
    b, t, _ = q.shape
    n_maps = 4 if is_diff else 2
    col0 = 0 if is_diff else N_PAIRS
    rows = n_maps * tq
    if is_diff:
        qaug_spec = pl.BlockSpec((1, tq, LANES), lambda i, p, j, s: (0, j, 0))
        kaug_spec = pl.BlockSpec((1, t, LANES), lambda i, p, j, s: (0, 0, 0))
    else:
        qaug_spec = pl.BlockSpec((1, tq, LANES), lambda i, p, j, s: (i, j, p))
        kaug_spec = pl.BlockSpec((1, t, LANES), lambda i, p, j, s: (i, 0, p))
    grid_spec = pltpu.PrefetchScalarGridSpec(
        num_scalar_prefetch=1,
        grid=(b, N_PAIRS, t // tq),
        in_specs=[pl.BlockSpec((1, tq, LANES), lambda i, p, j, s: (i, j, col0 + p)),
                  pl.BlockSpec((1, t, LANES), lambda i, p, j, s: (i, 0, col0 + p)),
                  pl.BlockSpec((1, t, LANES), lambda i, p, j, s: (i, 0, col0 + p)),
                  qaug_spec, kaug_spec,
                  pl.BlockSpec(lam.shape, lambda i, p, j, s: (0, 0)),
                  pl.BlockSpec(subln.shape, lambda i, p, j, s: (0, 0))],
        out_specs=pl.BlockSpec((1, tq, LANES), lambda i, p, j, s: (i, j, p)),
        scratch_shapes=[pltpu.VMEM((rows, 2 * LANES), BF16),
                        pltpu.VMEM((rows, LANES), F32),
                        pltpu.VMEM((rows, 2 * LANES), F32)],
    )
    return pl.pallas_call(
        functools.partial(_attn_kernel, is_diff=is_diff, tq=tq, tk=tk, rg=rg),
        grid_spec=grid_spec,
        out_shape=jax.ShapeDtypeStruct((b, t, HEADS_W), BF16),
        compiler_params=pltpu.CompilerParams(dimension_semantics=("arbitrary",) * 3,
                                             vmem_limit_bytes=VMEM_LIMIT),
        name="attn_diff" if is_diff else "attn_fox",
    )(slopes, q, k, v, qaug, kaug, lam, subln)


def _decode_kernel(pt_ref, qd_ref, qf_ref, knd_ref, vnd_ref, knf_ref, vnf_ref, lfn_ref, slope_ref,
                   lam_ref, subln_ref, usuf_ref, *rest, pages_per_step, n_pages, page):
    g = pages_per_step
    dk_refs, dv_refs = rest[0:g], rest[g:2 * g]
    fk_refs, fv_refs, lf_refs = rest[2 * g:3 * g], rest[3 * g:4 * g], rest[4 * g:5 * g]
    od_ref, of_ref = rest[5 * g], rest[5 * g + 1]
    qbd_ref, qbf_ref, md_ref, ld_ref, accd_ref, mf_ref, lsum_ref, accf_ref, tail_ref = rest[5 * g + 2:]
    del pt_ref
    j = pl.program_id(1)
    nh = N_HEADS
    w = HEADS_W
    sub = lax.broadcasted_iota(jnp.int32, (2 * nh, w), 0)
    lane = lax.broadcasted_iota(jnp.int32, (2 * nh, w), 1)
    own = (lane // HEAD_DIM) == (sub % nh)
    own8 = own[:nh]

    @pl.when(j == 0)
    def _():
        half = ((lane % HEAD_DIM) < DIFF_QK) == (sub < nh)
        qbd = jnp.where(own & half, jnp.broadcast_to(qd_ref[...].astype(F32), (2 * nh, w)), 0.0)
        qbf = jnp.where(own8, jnp.broadcast_to(qf_ref[...].astype(F32), (nh, w)), 0.0)
        qbd_ref[...] = qbd.astype(BF16)
        qbf_ref[...] = qbf.astype(BF16)
        md_ref[...] = jnp.sum(qbd * knd_ref[...], axis=-1, keepdims=True)
        ld_ref[...] = jnp.ones_like(ld_ref)
        accd_ref[...] = jnp.broadcast_to(vnd_ref[...], (2 * nh, w))
        mf_ref[...] = jnp.sum(qbf * knf_ref[...], axis=-1, keepdims=True)
        lsum_ref[...] = jnp.ones_like(lsum_ref)
        accf_ref[...] = jnp.broadcast_to(vnf_ref[...], (nh, w))
        tail_ref[...] = jnp.zeros_like(tail_ref)

    past = n_pages * page
    pos_in_page = lax.broadcasted_iota(jnp.int32, (2 * nh, page), 1)
    slope2 = jnp.concatenate([slope_ref[...], slope_ref[...]], axis=0) * LOG2E
    lf_new = lfn_ref[...] * LOG2E
    usuf = usuf_ref[...]
    tail = tail_ref[...]
    sd, sf = [], []
    for i in range(g):
        logical = n_pages - 1 - (j * g + i)
        s = _dot(qbd_ref[...], dk_refs[i][...].astype(BF16))
        dist = (past - (logical * page + pos_in_page)).astype(F32)
        sd.append(s - slope2 * dist)
        s = _dot(qbf_ref[...], fk_refs[i][...].astype(BF16))
        hi, mid, lo = _split3(lf_refs[i][...] * LOG2E)
        suf = _dot(hi.astype(BF16), usuf) + _dot(mid.astype(BF16), usuf) + _dot(lo.astype(BF16), usuf)
        sf.append(s + suf[:, :page] + tail + lf_new)
        tail = tail + suf[:, page:]
    tail_ref[...] = tail

    def online(parts, m_ref, l_ref, acc_ref, v_refs):
        s = jnp.concatenate(parts, axis=1)
        m_prev = m_ref[...]
        m_new = jnp.maximum(m_prev, jnp.max(s, axis=-1, keepdims=True))
        p = jnp.exp2(s - m_new)
        alpha = jnp.exp2(m_prev - m_new)
        l_ref[...] = alpha * l_ref[...] + jnp.sum(p, axis=-1, keepdims=True)
        pv = None
        for i in range(g):
            part = _dot_nt(p[:, i * page:(i + 1) * page].astype(BF16), v_refs[i][...].astype(BF16))
            pv = part if pv is None else pv + part
        acc_ref[...] = alpha * acc_ref[...] + pv
        m_ref[...] = m_new

    online(sd, md_ref, ld_ref, accd_ref, dv_refs)
    online(sf, mf_ref, lsum_ref, accf_ref, fv_refs)

    @pl.when(j == pl.num_programs(1) - 1)
    def _():
        o_f = jnp.where(own8, accf_ref[...] / lsum_ref[...], 0.0)
        of_ref[...] = jnp.sum(o_f, axis=0, keepdims=True)
        lam = _lambda_full(lam_ref)
        a = accd_ref[...] / ld_ref[...]
        o = jnp.where(own8, a[:nh] - lam * a[nh:], 0.0)
        ms = jnp.sum(o * o, axis=-1, keepdims=True) * (1.0 / HEAD_DIM)
        o = o * lax.rsqrt(ms + RMS_EPS)
        od_ref[...] = jnp.sum(o, axis=0, keepdims=True) * subln_ref[...] * (1.0 - LAMBDA_INIT)


def _suffix_matrix(page):
    u = np.ones((page, 2 * page), np.float32)
    u[:, :page] = np.arange(page)[:, None] > np.arange(page)[None, :]
    return jnp.asarray(u, BF16)


def _attend_decode(qd, qf, knd, vnd, knf, vnf, lf_new, caches, page_table, slopes, lam, subln, pages_per_step):
    cdk, cdv, cfk, cfv, clf = caches
    nb, n_pages = page_table.shape
    page = cdk.shape[-1]
    assert page == LANES
    g = min(pages_per_step, n_pages)
    assert n_pages % g == 0
    usuf = _suffix_matrix(page)
    nh = N_HEADS
    w = HEADS_W

    def page_spec(i, rows):
        return pl.BlockSpec((None, rows, page),
                            lambda b, j, pt: (pt[b * n_pages + n_pages - 1 - (j * g + i)], 0, 0))

    per_b = lambda shape: pl.BlockSpec((None,) + shape, lambda b, j, pt: (b,) + (0,) * len(shape))
    const = lambda a: pl.BlockSpec(a.shape, lambda b, j, pt: (0,) * a.ndim)
    in_specs = ([per_b((1, w))] * 6 + [per_b((nh, 1)), const(slopes), const(lam), const(subln), const(usuf)]
                + [page_spec(i, w) for _ in range(4) for i in range(g)]
                + [page_spec(i, nh) for i in range(g)])
    scratch = [pltpu.VMEM((2 * nh, w), BF16), pltpu.VMEM((nh, w), BF16),
               pltpu.VMEM((2 * nh, 1), F32), pltpu.VMEM((2 * nh, 1), F32), pltpu.VMEM((2 * nh, w), F32),
               pltpu.VMEM((nh, 1), F32), pltpu.VMEM((nh, 1), F32), pltpu.VMEM((nh, w), F32),
               pltpu.VMEM((nh, page), F32)]
    grid_spec = pltpu.PrefetchScalarGridSpec(
        num_scalar_prefetch=1, grid=(nb, n_pages // g), in_specs=in_specs,
        out_specs=[per_b((1, w))] * 2, scratch_shapes=scratch)
    return pl.pallas_call(
        functools.partial(_decode_kernel, pages_per_step=g, n_pages=n_pages, page=page),
        grid_spec=grid_spec,
        out_shape=[jax.ShapeDtypeStruct((nb, 1, w), F32)] * 2,
        compiler_params=pltpu.CompilerParams(dimension_semantics=("arbitrary", "arbitrary"),
                                             vmem_limit_bytes=VMEM_LIMIT),
        name="attn_decode",
    )(page_table.reshape(-1), qd, qf, knd, vnd, knf, vnf, lf_new, slopes, lam, subln, usuf,
      *([cdk] * g + [cdv] * g + [cfk] * g + [cfv] * g + [clf] * g))


def _post_kernel(h_ref, o_ref, p_ref, wo_ref, g2_ref, wg_ref, wu_ref, wd_ref, gp_ref, wpg_ref, bpg_ref,
                 wpp_ref, gf_ref, y_ref, *, chunk):
    h = h_ref[...] + _dot(o_ref[...], wo_ref[...])
    hn = _rms(h, g2_ref[...]).astype(BF16)
    h = h + 0.5 * _swiglu(hn, wg_ref, wu_ref, wd_ref, chunk)
    hn = _rms(h, gp_ref[...]).astype(BF16)
    gate = jax.nn.sigmoid(_dot(hn, wpg_ref[...]) + bpg_ref[...])
    h = h + gate * _dot(p_ref[...].astype(BF16), wpp_ref[...])
    y_ref[...] = _rms(h, gf_ref[...])


def _post(h, o, p, wo, g2, wg, wu, wd, gp, wpg, bpg, wpp, gf, tm):
    n, d = h.shape
    row = lambda width: pl.BlockSpec((tm, width), lambda i: (i, 0))
    consts = (wo, g2, wg, wu, wd, gp, wpg, bpg, wpp, gf)
    return pl.pallas_call(
        functools.partial(_post_kernel, chunk=_ff_chunk(wg.shape[1])),
        grid=(n // tm,),
        in_specs=[row(d), row(o.shape[1]), row(p.shape[1])] + [_const_spec(c.shape) for c in consts],
        out_specs=row(d),
        out_shape=jax.ShapeDtypeStruct((n, d), F32),
        compiler_params=pltpu.CompilerParams(dimension_semantics=("arbitrary",), vmem_limit_bytes=VMEM_LIMIT),
        name="post",
    )(h, o, p, *consts)


def _row_tile(n, target):
    tm = min(n, target)
    assert n % tm == 0 and (tm % 8 == 0 or tm == n), (n, tm)
    return tm


def kernel(x_prompt, x_sample, cache_diff_k, cache_diff_v, cache_fox_k, cache_fox_v, cache_fox_logf, page_table, p_prompt, p_sample, ffn1_norm, ffn1_w_gate, ffn1_w_up, ffn1_w_down, mix_norm, w_in, b_forget, lambda_q1, lambda_k1, lambda_q2, lambda_k2, diff_subln, w_out, ffn2_norm, ffn2_w_gate, ffn2_w_up, ffn2_w_down, ple_norm, w_ple_gate, b_ple_gate, w_ple_proj, final_norm):
    depth = w_in.shape[0]
    assert depth == 1, "single-layer step only"
    d = x_prompt.shape[-1]
    w = HEADS_W
    assert w_in.shape[-1] == 6 * w + N_HEADS and w_out.shape[1:] == (2 * w, d)
    bf = lambda a: a.astype(BF16)
    row = lambda a: a.reshape(1, -1)

    wg1, wu1, wd1 = bf(ffn1_w_gate[0]), bf(ffn1_w_up[0]), bf(ffn1_w_down[0])
    wg2, wu2, wd2 = bf(ffn2_w_gate[0]), bf(ffn2_w_up[0]), bf(ffn2_w_down[0])
    w_qkv = bf(w_in[0, :, :6 * w])
    w_f = jnp.pad(bf(w_in[0, :, 6 * w:]), ((0, 0), (0, LANES - N_HEADS)))
    b_f = jnp.pad(b_forget[0], (0, LANES - N_HEADS)).reshape(1, LANES)
    wo, wpg, wpp = bf(w_out[0]), bf(w_ple_gate[0]), bf(w_ple_proj[0])
    lam = jnp.stack([lambda_q1[0], lambda_k1[0], lambda_q2[0], lambda_k2[0]])
    subln_pair = jnp.tile(diff_subln[0], 2).reshape(1, LANES)
    slopes = jnp.asarray(_alibi_slopes())

    def dense_pre(x3, tm):
        b, t, _ = x3.shape
        h1 = _ffn(x3.reshape(b * t, d), row(ffn1_norm[0]), wg1, wu1, wd1, tm)
        outs = _project(h1.reshape(b, t, d), row(mix_norm[0]), w_qkv, w_f, b_f, tm)
        return h1, outs

    def dense_post(h1, o, p2, tm):
        return _post(h1, o, p2, wo, row(ffn2_norm[0]), wg2, wu2, wd2, row(ple_norm[0]), wpg,
                     row(b_ple_gate[0]), wpp, row(final_norm), tm)

    def heads(a, b, t):
        return a.reshape(1, b, t, N_HEADS, HEAD_DIM)

    b, t, _ = x_prompt.shape
    tm = _row_tile(t, 512)
    h1, (q, k, v, kd, vd, kf, vf, lf, qaug, kaug) = dense_pre(x_prompt, tm)
    tq = _row_tile(t, 512)
    q_tab, k_tab = _alibi_tables(t, tq)
    o_d = _attend_prompt(q, k, v, q_tab[None], k_tab[None], slopes, lam, subln_pair, is_diff=True, tq=tq, tk=tq, rg=min(tq, ATTN_ROW_GROUP))
    o_f = _attend_prompt(q, k, v, qaug, kaug, slopes, lam, subln_pair, is_diff=False, tq=tq, tk=tq, rg=min(tq, ATTN_ROW_GROUP))
    o = jnp.concatenate([o_d, o_f], axis=-1).reshape(b * t, 2 * w)
    y_prompt = dense_post(h1, o, p_prompt[0].reshape(b * t, -1), tm).reshape(b, t, d)
    prompt_new = (heads(kd, b, t), heads(vd, b, t), heads(kf, b, t), heads(vf, b, t), lf.reshape(1, b, t, N_HEADS))

    nb, ts, _ = x_sample.shape
    assert ts == 1, "decode kernel handles one query per row"
    xs = x_sample.reshape(1, nb, d)
    h1s, (qs, _, _, kds, vds, kfs, vfs, lfs, _, _) = dense_pre(xs, nb)
    flat = lambda a: a.reshape(nb, 1, w)
    kv_view = lambda c: jnp.transpose(c[0], (0, 2, 3, 1)).reshape(c.shape[1], w, c.shape[2])
    caches = (kv_view(cache_diff_k), kv_view(cache_diff_v), kv_view(cache_fox_k), kv_view(cache_fox_v),
              jnp.transpose(cache_fox_logf[0], (0, 2, 1)))
    o_ds, o_fs = _attend_decode(flat(qs[0, :, :w]), flat(qs[0, :, w:]), flat(kds), flat(vds), flat(kfs), flat(vfs),
                                lfs.reshape(nb, N_HEADS, 1), caches, page_table, slopes.reshape(N_HEADS, 1),
                                lam, jnp.tile(diff_subln[0], N_HEADS).reshape(1, w),
                                pages_per_step=DECODE_PAGES_PER_STEP)
    o_s = bf(jnp.concatenate([o_ds.reshape(nb, w), o_fs.reshape(nb, w)], axis=-1))
    y_sample = dense_post(h1s, o_s, p_sample[0].reshape(nb, -1), nb).reshape(nb, 1, d)
    sample_new = (heads(kds, nb, 1), heads(vds, nb, 1), heads(kfs, nb, 1), heads(vfs, nb, 1),
                  lfs.reshape(1, nb, 1, N_HEADS))

    return (y_prompt, y_sample) + prompt_new + sample_new
```

```python
import functools
import math

import numpy as np
import jax
import jax.numpy as jnp
from jax import lax
from jax.experimental import pallas as pl
from jax.experimental.pallas import tpu as pltpu

F32 = jnp.float32
BF16 = jnp.bfloat16

HEAD_DIM = 64
N_HEADS = 8
HEADS_W = N_HEADS * HEAD_DIM
DIFF_QK = HEAD_DIM // 2
LANES = 128
N_PAIRS = HEADS_W // LANES
AUG_PER_HEAD = 6
RMS_EPS = 1e-6
LOG2E = 1.4426950408889634
LAMBDA_INIT = 0.8 - 0.6 * math.exp(-0.3 * 0)
NEG_INF = float("-inf")
VMEM_LIMIT = 56 * 1024 * 1024
DECODE_PAGES_PER_STEP = 8
ATTN_ROW_GROUP = 256
ATTN_KV_CHUNK = 512
ATTN_CHAINS_PER_TRIP = 32


def _rms(x, g):
    return x * lax.rsqrt(jnp.mean(x * x, axis=-1, keepdims=True) + RMS_EPS) * g


def _dot(a, b):
    return jnp.dot(a, b, preferred_element_type=F32)


def _dot_nt(a, b):
    return lax.dot_general(a, b, (((1,), (1,)), ((), ())), preferred_element_type=F32)


def _split3(x):
    hi = x.astype(BF16).astype(F32)
    r = x - hi
    mid = r.astype(BF16).astype(F32)
    lo = (r - mid).astype(BF16).astype(F32)
    return hi, mid, lo


def _const_spec(shape):
    nd = len(shape)
    return pl.BlockSpec(shape, lambda *_: (0,) * nd, pipeline_mode=pl.Buffered(1))


def _swiglu(hn, wg_ref, wu_ref, wd_ref, chunk):
    d_ff = wg_ref.shape[1]
    out = None
    for c0 in range(0, d_ff, chunk):
        g = _dot(hn, wg_ref[:, c0:c0 + chunk])
        u = _dot(hn, wu_ref[:, c0:c0 + chunk])
        a = (g * jax.nn.sigmoid(g) * u).astype(BF16)
        part = _dot(a, wd_ref[c0:c0 + chunk, :])
        out = part if out is None else out + part
    return out


def _ff_chunk(d_ff):
    return d_ff


def _ffn_kernel(x_ref, g_ref, wg_ref, wu_ref, wd_ref, h_ref, *, chunk):
    x = x_ref[...]
    hn = _rms(x, g_ref[...]).astype(BF16)
    h_ref[...] = x + 0.5 * _swiglu(hn, wg_ref, wu_ref, wd_ref, chunk)


def _ffn(x, g, wg, wu, wd, tm):
    n, d = x.shape
    d_ff = wg.shape[1]
    return pl.pallas_call(
        functools.partial(_ffn_kernel, chunk=_ff_chunk(d_ff)),
        grid=(n // tm,),
        in_specs=[pl.BlockSpec((tm, d), lambda i: (i, 0)),
                  _const_spec((1, d)), _const_spec((d, d_ff)), _const_spec((d, d_ff)), _const_spec((d_ff, d))],
        out_specs=pl.BlockSpec((tm, d), lambda i: (i, 0)),
        out_shape=jax.ShapeDtypeStruct((n, d), F32),
        compiler_params=pltpu.CompilerParams(dimension_semantics=("arbitrary",), vmem_limit_bytes=VMEM_LIMIT),
        name="ffn1",
    )(x, g, wg, wu, wd)


def _proj_kernel(h_ref, g_ref, win_ref, wf_ref, bf_ref, tri_ref, selq_ref, selk_ref,
                 q_ref, k_ref, v_ref, kd_ref, vd_ref, kf_ref, vf_ref, lf_ref, qaug_ref, kaug_ref,
                 carry_ref, *, q_scale_d, q_scale_f):
    t = pl.program_id(1)
    w = HEADS_W

    @pl.when(t == 0)
    def _():
        carry_ref[...] = jnp.zeros_like(carry_ref)

    hn = _rms(h_ref[0], g_ref[...]).astype(BF16)
    z = _dot(hn, win_ref[...])
    kd, vd = z[:, w:2 * w], z[:, 2 * w:3 * w]
    kf, vf = z[:, 4 * w:5 * w], z[:, 5 * w:6 * w]
    kd_ref[0], vd_ref[0], kf_ref[0], vf_ref[0] = kd, vd, kf, vf
    q_ref[0, :, :w] = (z[:, :w] * q_scale_d).astype(BF16)
    q_ref[0, :, w:] = (z[:, 3 * w:4 * w] * q_scale_f).astype(BF16)
    k_ref[0, :, :w] = kd.astype(BF16)
    k_ref[0, :, w:] = kf.astype(BF16)
    v_ref[0, :, :w] = vd.astype(BF16)
    v_ref[0, :, w:] = vf.astype(BF16)

    x = _dot(hn, wf_ref[...]) + bf_ref[...]
    lane = lax.broadcasted_iota(jnp.int32, x.shape, 1)
    lf = -(jnp.maximum(-x, 0.0) + jnp.log1p(jnp.exp(-jnp.abs(x))))
    lf = jnp.where(lane < N_HEADS, lf, 0.0)
    lf_ref[0] = lf[:, :N_HEADS]

    hi, mid, lo = _split3(lf * LOG2E)
    tri = tri_ref[...]
    c = (_dot(tri, hi.astype(BF16)) + _dot(tri, mid.astype(BF16)) + _dot(tri, lo.astype(BF16))
         + carry_ref[...])
    tm = c.shape[0]
    carry_ref[...] = c[tm - 1:tm, :]

    chi, cmid, clo = _split3(c)
    packed = (chi + pltpu.roll(cmid, N_HEADS, 1) + pltpu.roll(clo, 2 * N_HEADS, 1)
              + jnp.where(lane == 3 * N_HEADS, 1.0, 0.0)).astype(BF16)
    qaug_ref[0] = _dot(packed, selq_ref[...]).astype(BF16)
    kaug_ref[0] = _dot(packed, selk_ref[...]).astype(BF16)


def _aug_select_matrices():
    selq = np.zeros((LANES, N_PAIRS * LANES), np.float32)
    selk = np.zeros((LANES, N_PAIRS * LANES), np.float32)
    one = 3 * N_HEADS
    for p in range(N_PAIRS):
        for s in range(2):
            head = 2 * p + s
            base = p * LANES + s * AUG_PER_HEAD
            for piece in range(3):
                selq[piece * N_HEADS + head, base + piece] = 1.0
                selq[one, base + 3 + piece] = 1.0
                selk[one, base + piece] = 1.0
                selk[piece * N_HEADS + head, base + 3 + piece] = -1.0
    return jnp.asarray(selq, BF16), jnp.asarray(selk, BF16)


def _project(h, g, w_in, w_f, b_f, tm):
    b, t, d = h.shape
    w = HEADS_W
    selq, selk = _aug_select_matrices()
    tri = jnp.asarray(np.tril(np.ones((tm, tm), np.float32)), BF16)
    row = lambda width: pl.BlockSpec((1, tm, width), lambda i, j: (i, j, 0))
    sds = lambda width, dt: jax.ShapeDtypeStruct((b, t, width), dt)
    kern = functools.partial(_proj_kernel,
                             q_scale_d=DIFF_QK ** -0.5 * LOG2E, q_scale_f=HEAD_DIM ** -0.5 * LOG2E)
    return pl.pallas_call(
        kern,
        grid=(b, t // tm),
        in_specs=[row(d), _const_spec((1, d)), _const_spec(w_in.shape), _const_spec(w_f.shape),
                  _const_spec((1, LANES)), _const_spec((tm, tm)), _const_spec(selq.shape), _const_spec(selk.shape)],
        out_specs=[row(2 * w), row(2 * w), row(2 * w), row(w), row(w), row(w), row(w),
                   row(N_HEADS), row(N_PAIRS * LANES), row(N_PAIRS * LANES)],
        out_shape=[sds(2 * w, BF16), sds(2 * w, BF16), sds(2 * w, BF16),
                   sds(w, F32), sds(w, F32), sds(w, F32), sds(w, F32),
                   sds(N_HEADS, F32), sds(N_PAIRS * LANES, BF16), sds(N_PAIRS * LANES, BF16)],
        scratch_shapes=[pltpu.VMEM((1, LANES), F32)],
        compiler_params=pltpu.CompilerParams(dimension_semantics=("arbitrary", "arbitrary"),
                                             vmem_limit_bytes=VMEM_LIMIT),
        name="project",
    )(h, g, w_in, w_f, b_f, tri, selq, selk)


def _alibi_kernel(q_ref, k_ref):
    rows = q_ref.shape[0]
    pos = (pl.program_id(0) * rows + lax.broadcasted_iota(jnp.int32, (rows, LANES), 0)).astype(F32)
    lane = lax.broadcasted_iota(jnp.int32, (rows, LANES), 1)
    hi, mid, lo = _split3(pos * LOG2E)
    pieces = jnp.where(lane % 3 == 0, hi, jnp.where(lane % 3 == 1, mid, lo))
    q_ref[...] = jnp.where(lane < 3, -pieces, jnp.where(lane < 6, 1.0, 0.0)).astype(BF16)
    k_ref[...] = jnp.where(lane < 3, 1.0, jnp.where(lane < 6, pieces, 0.0)).astype(BF16)


def _alibi_tables(n, rows):
    return pl.pallas_call(
        _alibi_kernel,
        grid=(n // rows,),
        out_specs=[pl.BlockSpec((rows, LANES), lambda i: (i, 0))] * 2,
        out_shape=[jax.ShapeDtypeStruct((n, LANES), BF16)] * 2,
        name="alibi_tables",
    )()


def _alibi_slopes():
    slopes = 2.0 ** (-8.0 * np.arange(1, N_HEADS + 1, dtype=np.float64) / N_HEADS)
    assert np.all(np.log2(slopes) == np.round(np.log2(slopes))), "bias columns assume power-of-two slopes"
    return slopes.astype(np.float32)


def _lambda_full(lam_ref):
    l = lam_ref[...]
    s1 = jnp.sum(l[0:1] * l[1:2], axis=-1, keepdims=True)
    s2 = jnp.sum(l[2:3] * l[3:4], axis=-1, keepdims=True)
    return jnp.exp(s1) - jnp.exp(s2) + LAMBDA_INIT


def _head_rms(o, subln_row, lane):
    sq = o * o
    first = lane < HEAD_DIM
    sum_a = jnp.sum(jnp.where(first, sq, 0.0), axis=-1, keepdims=True)
    sum_b = jnp.sum(jnp.where(first, 0.0, sq), axis=-1, keepdims=True)
    ms = jnp.where(first, sum_a, sum_b) * (1.0 / HEAD_DIM)
    return o * lax.rsqrt(ms + RMS_EPS) * subln_row


def _attn_kernel(slope_ref, q_ref, k_ref, v_ref, qaug_ref, kaug_ref, lam_ref, subln_ref, o_ref,
                 qs_ref, m_ref, acc_ref, *, is_diff, tq, tk, rg, unroll):
    n_maps = 4 if is_diff else 2
    rows = n_maps * tq
    pair = pl.program_id(1)
    qi = pl.program_id(2)
    lane = lax.broadcasted_iota(jnp.int32, (tq, LANES), 1)

    q = q_ref[0].astype(F32)
    qa = qaug_ref[0].astype(F32)
    map_w = LANES // n_maps
    for m in range(n_maps):
        qm = jnp.where((lane >= m * map_w) & (lane < (m + 1) * map_w), q, 0.0)
        if is_diff:
            aug = qa * slope_ref[2 * pair + m // 2]
        else:
            aug = jnp.where((lane >= m * AUG_PER_HEAD) & (lane < (m + 1) * AUG_PER_HEAD), qa, 0.0)
        qs_ref[m * tq:(m + 1) * tq, :LANES] = qm.astype(BF16)
        qs_ref[m * tq:(m + 1) * tq, LANES:] = aug.astype(BF16)

    m_ref[...] = jnp.full_like(m_ref, NEG_INF)
    acc_ref[...] = jnp.zeros_like(acc_ref)
    ones = jnp.ones((tk, LANES), BF16)

    def step(j, diag=None):
        start = pl.multiple_of(j * tk, tk)
        kfull = jnp.concatenate([k_ref[0, pl.ds(start, tk), :], kaug_ref[0, pl.ds(start, tk), :]], axis=1)
        vfull = jnp.concatenate([v_ref[0, pl.ds(start, tk), :], ones], axis=1)
        for r0 in range(0, rows, rg):
            rs = slice(r0, r0 + rg)
            kw = tk
            if diag is not None:
                kw = max(0, min(tk, r0 % tq + rg - diag * tk))
                if kw == 0:
                    continue
            s = _dot_nt(qs_ref[rs, :], kfull[:kw])
            if diag is not None:
                r = lax.broadcasted_iota(jnp.int32, (rg, kw), 0) + (r0 % tq + qi * tq)
                c = lax.broadcasted_iota(jnp.int32, (rg, kw), 1) + start
                s = jnp.where(c <= r, s, NEG_INF)
            m_prev = m_ref[rs, :]
            m_new = jnp.maximum(m_prev, jnp.max(s, axis=1, keepdims=True))
            p = jnp.exp2(s - jnp.concatenate([m_new] * (kw // LANES), axis=1)).astype(BF16)
            alpha = jnp.exp2(m_prev - m_new)
            acc_ref[rs, :] = acc_ref[rs, :] * jnp.concatenate([alpha, alpha], axis=1) + _dot(p, vfull[:kw])
            m_ref[rs, :] = m_new

    n_full = (qi * tq) // tk

    def body(jj, carry):
        for u in range(unroll):
            step(unroll * jj + u)
        return carry

    lax.fori_loop(0, n_full // unroll, body, 0)

    done = (n_full // unroll) * unroll
    part = unroll // 2
    while part >= 1:
        @pl.when((n_full - done) % (2 * part) >= part)
        def _(done=done, part=part):
            for u in range(part):
                step(done + u)
        done = done + jnp.where((n_full - done) % (2 * part) >= part, part, 0)
        part //= 2

    for d in range(tq // tk):
        step(n_full + d, diag=d)

    def normalized(m):
        a = acc_ref[m * tq:(m + 1) * tq, :]
        return a[:, :LANES] / a[:, LANES:]

    first = lane < HEAD_DIM
    if is_diff:
        lam = _lambda_full(lam_ref)
        o = jnp.where(first, normalized(0) - lam * normalized(1), normalized(2) - lam * normalized(3))
        o = _head_rms(o, subln_ref[...], lane) * (1.0 - LAMBDA_INIT)
    else:
        o = jnp.where(first, normalized(0), normalized(1))
    o_ref[0] = o.astype(o_ref.dtype)


def _attend_prompt(q, k, v, qaug, kaug, slopes, lam, subln, *, is_diff, tq, tk, rg):
    b, t, _ = q.shape
    n_maps = 4 if is_diff else 2
    col0 = 0 if is_diff else N_PAIRS
    rows = n_maps * tq
    assert tq % tk == 0 and tq % rg == 0
    unroll = max(1, ATTN_CHAINS_PER_TRIP // (rows // rg))
    assert unroll & (unroll - 1) == 0
    if is_diff:
        qaug_spec = pl.BlockSpec((1, tq, LANES), lambda i, p, j, s: (0, j, 0))
        kaug_spec = pl.BlockSpec((1, t, LANES), lambda i, p, j, s: (0, 0, 0))
    else:
        qaug_spec = pl.BlockSpec((1, tq, LANES), lambda i, p, j, s: (i, j, p))
        kaug_spec = pl.BlockSpec((1, t, LANES), lambda i, p, j, s: (i, 0, p))
    grid_spec = pltpu.PrefetchScalarGridSpec(
        num_scalar_prefetch=1,
        grid=(b, N_PAIRS, t // tq),
        in_specs=[pl.BlockSpec((1, tq, LANES), lambda i, p, j, s: (i, j, col0 + p)),
                  pl.BlockSpec((1, t, LANES), lambda i, p, j, s: (i, 0, col0 + p)),
                  pl.BlockSpec((1, t, LANES), lambda i, p, j, s: (i, 0, col0 + p)),
                  qaug_spec, kaug_spec,
                  pl.BlockSpec(lam.shape, lambda i, p, j, s: (0, 0)),
                  pl.BlockSpec(subln.shape, lambda i, p, j, s: (0, 0))],
        out_specs=pl.BlockSpec((1, tq, LANES), lambda i, p, j, s: (i, j, p)),
        scratch_shapes=[pltpu.VMEM((rows, 2 * LANES), BF16),
                        pltpu.VMEM((rows, LANES), F32),
                        pltpu.VMEM((rows, 2 * LANES), F32)],
    )
    return pl.pallas_call(
        functools.partial(_attn_kernel, is_diff=is_diff, tq=tq, tk=tk, rg=rg, unroll=unroll),
        grid_spec=grid_spec,
        out_shape=jax.ShapeDtypeStruct((b, t, HEADS_W), BF16),
        compiler_params=pltpu.CompilerParams(dimension_semantics=("arbitrary",) * 3,
                                             vmem_limit_bytes=VMEM_LIMIT),
        name="attn_diff" if is_diff else "attn_fox",
    )(slopes, q, k, v, qaug, kaug, lam, subln)


def _decode_kernel(pt_ref, qd_ref, qf_ref, knd_ref, vnd_ref, knf_ref, vnf_ref, lfn_ref, slope_ref,
                   lam_ref, subln_ref, usuf_ref, *rest, pages_per_step, n_pages, page):
    g = pages_per_step
    dk_refs, dv_refs = rest[0:g], rest[g:2 * g]
    fk_refs, fv_refs, lf_refs = rest[2 * g:3 * g], rest[3 * g:4 * g], rest[4 * g:5 * g]
    od_ref, of_ref = rest[5 * g], rest[5 * g + 1]
    qbd_ref, qbf_ref, md_ref, ld_ref, accd_ref, mf_ref, lsum_ref, accf_ref, tail_ref = rest[5 * g + 2:]
    del pt_ref
    j = pl.program_id(1)
    nh = N_HEADS
    w = HEADS_W
    sub = lax.broadcasted_iota(jnp.int32, (2 * nh, w), 0)
    lane = lax.broadcasted_iota(jnp.int32, (2 * nh, w), 1)
    own = (lane // HEAD_DIM) == (sub % nh)
    own8 = own[:nh]

    @pl.when(j == 0)
    def _():
        half = ((lane % HEAD_DIM) < DIFF_QK) == (sub < nh)
        qbd = jnp.where(own & half, jnp.broadcast_to(qd_ref[...].astype(F32), (2 * nh, w)), 0.0)
        qbf = jnp.where(own8, jnp.broadcast_to(qf_ref[...].astype(F32), (nh, w)), 0.0)
        qbd_ref[...] = qbd.astype(BF16)
        qbf_ref[...] = qbf.astype(BF16)
        md_ref[...] = jnp.sum(qbd * knd_ref[...], axis=-1, keepdims=True)
        ld_ref[...] = jnp.ones_like(ld_ref)
        accd_ref[...] = jnp.broadcast_to(vnd_ref[...], (2 * nh, w))
        mf_ref[...] = jnp.sum(qbf * knf_ref[...], axis=-1, keepdims=True)
        lsum_ref[...] = jnp.ones_like(lsum_ref)
        accf_ref[...] = jnp.broadcast_to(vnf_ref[...], (nh, w))
        tail_ref[...] = jnp.zeros_like(tail_ref)

    past = n_pages * page
    pos_in_page = lax.broadcasted_iota(jnp.int32, (2 * nh, page), 1)
    slope2 = jnp.concatenate([slope_ref[...], slope_ref[...]], axis=0) * LOG2E
    lf_new = lfn_ref[...] * LOG2E
    usuf = usuf_ref[...]
    tail = tail_ref[...]
    sd, sf = [], []
    for i in range(g):
        logical = n_pages - 1 - (j * g + i)
        s = _dot(qbd_ref[...], dk_refs[i][...].astype(BF16))
        dist = (past - (logical * page + pos_in_page)).astype(F32)
        sd.append(s - slope2 * dist)
        s = _dot(qbf_ref[...], fk_refs[i][...].astype(BF16))
        hi, mid, lo = _split3(lf_refs[i][...] * LOG2E)
        suf = _dot(hi.astype(BF16), usuf) + _dot(mid.astype(BF16), usuf) + _dot(lo.astype(BF16), usuf)
        sf.append(s + suf[:, :page] + tail + lf_new)
        tail = tail + suf[:, page:]
    tail_ref[...] = tail

    def online(parts, m_ref, l_ref, acc_ref, v_refs):
        s = jnp.concatenate(parts, axis=1)
        m_prev = m_ref[...]
        m_new = jnp.maximum(m_prev, jnp.max(s, axis=-1, keepdims=True))
        p = jnp.exp2(s - m_new)
        alpha = jnp.exp2(m_prev - m_new)
        l_ref[...] = alpha * l_ref[...] + jnp.sum(p, axis=-1, keepdims=True)
        pv = None
        for i in range(g):
            part = _dot_nt(p[:, i * page:(i + 1) * page].astype(BF16), v_refs[i][...].astype(BF16))
            pv = part if pv is None else pv + part
        acc_ref[...] = alpha * acc_ref[...] + pv
        m_ref[...] = m_new

    online(sd, md_ref, ld_ref, accd_ref, dv_refs)
    online(sf, mf_ref, lsum_ref, accf_ref, fv_refs)

    @pl.when(j == pl.num_programs(1) - 1)
    def _():
        o_f = jnp.where(own8, accf_ref[...] / lsum_ref[...], 0.0)
        of_ref[...] = jnp.sum(o_f, axis=0, keepdims=True)
        lam = _lambda_full(lam_ref)
        a = accd_ref[...] / ld_ref[...]
        o = jnp.where(own8, a[:nh] - lam * a[nh:], 0.0)
        ms = jnp.sum(o * o, axis=-1, keepdims=True) * (1.0 / HEAD_DIM)
        o = o * lax.rsqrt(ms + RMS_EPS)
        od_ref[...] = jnp.sum(o, axis=0, keepdims=True) * subln_ref[...] * (1.0 - LAMBDA_INIT)


def _suffix_matrix(page):
    u = np.ones((page, 2 * page), np.float32)
    u[:, :page] = np.arange(page)[:, None] > np.arange(page)[None, :]
    return jnp.asarray(u, BF16)


def _attend_decode(qd, qf, knd, vnd, knf, vnf, lf_new, caches, page_table, slopes, lam, subln, pages_per_step):
    cdk, cdv, cfk, cfv, clf = caches
    nb, n_pages = page_table.shape
    page = cdk.shape[-1]
    assert page == LANES
    g = min(pages_per_step, n_pages)
    assert n_pages % g == 0
    usuf = _suffix_matrix(page)
    nh = N_HEADS
    w = HEADS_W

    def page_spec(i, rows):
        return pl.BlockSpec((None, rows, page),
                            lambda b, j, pt: (pt[b * n_pages + n_pages - 1 - (j * g + i)], 0, 0))

    per_b = lambda shape: pl.BlockSpec((None,) + shape, lambda b, j, pt: (b,) + (0,) * len(shape))
    const = lambda a: pl.BlockSpec(a.shape, lambda b, j, pt: (0,) * a.ndim)
    in_specs = ([per_b((1, w))] * 6 + [per_b((nh, 1)), const(slopes), const(lam), const(subln), const(usuf)]
                + [page_spec(i, w) for _ in range(4) for i in range(g)]
                + [page_spec(i, nh) for i in range(g)])
    scratch = [pltpu.VMEM((2 * nh, w), BF16), pltpu.VMEM((nh, w), BF16),
               pltpu.VMEM((2 * nh, 1), F32), pltpu.VMEM((2 * nh, 1), F32), pltpu.VMEM((2 * nh, w), F32),
               pltpu.VMEM((nh, 1), F32), pltpu.VMEM((nh, 1), F32), pltpu.VMEM((nh, w), F32),
               pltpu.VMEM((nh, page), F32)]
    grid_spec = pltpu.PrefetchScalarGridSpec(
        num_scalar_prefetch=1, grid=(nb, n_pages // g), in_specs=in_specs,
        out_specs=[per_b((1, w))] * 2, scratch_shapes=scratch)
    return pl.pallas_call(
        functools.partial(_decode_kernel, pages_per_step=g, n_pages=n_pages, page=page),
        grid_spec=grid_spec,
        out_shape=[jax.ShapeDtypeStruct((nb, 1, w), F32)] * 2,
        compiler_params=pltpu.CompilerParams(dimension_semantics=("arbitrary", "arbitrary"),
                                             vmem_limit_bytes=VMEM_LIMIT),
        name="attn_decode",
    )(page_table.reshape(-1), qd, qf, knd, vnd, knf, vnf, lf_new, slopes, lam, subln, usuf,
      *([cdk] * g + [cdv] * g + [cfk] * g + [cfv] * g + [clf] * g))


def _post_kernel(h_ref, o_ref, p_ref, wo_ref, g2_ref, wg_ref, wu_ref, wd_ref, gp_ref, wpg_ref, bpg_ref,
                 wpp_ref, gf_ref, y_ref, *, chunk):
    h = h_ref[...] + _dot(o_ref[...], wo_ref[...])
    hn = _rms(h, g2_ref[...]).astype(BF16)
    h = h + 0.5 * _swiglu(hn, wg_ref, wu_ref, wd_ref, chunk)
    hn = _rms(h, gp_ref[...]).astype(BF16)
    gate = jax.nn.sigmoid(_dot(hn, wpg_ref[...]) + bpg_ref[...])
    h = h + gate * _dot(p_ref[...].astype(BF16), wpp_ref[...])
    y_ref[...] = _rms(h, gf_ref[...])


def _post(h, o, p, wo, g2, wg, wu, wd, gp, wpg, bpg, wpp, gf, tm):
    n, d = h.shape
    row = lambda width: pl.BlockSpec((tm, width), lambda i: (i, 0))
    consts = (wo, g2, wg, wu, wd, gp, wpg, bpg, wpp, gf)
    return pl.pallas_call(
        functools.partial(_post_kernel, chunk=_ff_chunk(wg.shape[1])),
        grid=(n // tm,),
        in_specs=[row(d), row(o.shape[1]), row(p.shape[1])] + [_const_spec(c.shape) for c in consts],
        out_specs=row(d),
        out_shape=jax.ShapeDtypeStruct((n, d), F32),
        compiler_params=pltpu.CompilerParams(dimension_semantics=("arbitrary",), vmem_limit_bytes=VMEM_LIMIT),
        name="post",
    )(h, o, p, *consts)


def _row_tile(n, target):
    tm = min(n, target)
    assert n % tm == 0 and (tm % 8 == 0 or tm == n), (n, tm)
    return tm


def kernel(x_prompt, x_sample, cache_diff_k, cache_diff_v, cache_fox_k, cache_fox_v, cache_fox_logf, page_table, p_prompt, p_sample, ffn1_norm, ffn1_w_gate, ffn1_w_up, ffn1_w_down, mix_norm, w_in, b_forget, lambda_q1, lambda_k1, lambda_q2, lambda_k2, diff_subln, w_out, ffn2_norm, ffn2_w_gate, ffn2_w_up, ffn2_w_down, ple_norm, w_ple_gate, b_ple_gate, w_ple_proj, final_norm):
    depth = w_in.shape[0]
    assert depth == 1, "single-layer step only"
    d = x_prompt.shape[-1]
    w = HEADS_W
    assert w_in.shape[-1] == 6 * w + N_HEADS and w_out.shape[1:] == (2 * w, d)
    bf = lambda a: a.astype(BF16)
    row = lambda a: a.reshape(1, -1)

    wg1, wu1, wd1 = bf(ffn1_w_gate[0]), bf(ffn1_w_up[0]), bf(ffn1_w_down[0])
    wg2, wu2, wd2 = bf(ffn2_w_gate[0]), bf(ffn2_w_up[0]), bf(ffn2_w_down[0])
    w_qkv = bf(w_in[0, :, :6 * w])
    w_f = jnp.pad(bf(w_in[0, :, 6 * w:]), ((0, 0), (0, LANES - N_HEADS)))
    b_f = jnp.pad(b_forget[0], (0, LANES - N_HEADS)).reshape(1, LANES)
    wo, wpg, wpp = bf(w_out[0]), bf(w_ple_gate[0]), bf(w_ple_proj[0])
    lam = jnp.stack([lambda_q1[0], lambda_k1[0], lambda_q2[0], lambda_k2[0]])
    subln_pair = jnp.tile(diff_subln[0], 2).reshape(1, LANES)
    slopes = jnp.asarray(_alibi_slopes())

    def dense_pre(x3, tm):
        b, t, _ = x3.shape
        h1 = _ffn(x3.reshape(b * t, d), row(ffn1_norm[0]), wg1, wu1, wd1, tm)
        outs = _project(h1.reshape(b, t, d), row(mix_norm[0]), w_qkv, w_f, b_f, tm)
        return h1, outs

    def dense_post(h1, o, p2, tm):
        return _post(h1, o, p2, wo, row(ffn2_norm[0]), wg2, wu2, wd2, row(ple_norm[0]), wpg,
                     row(b_ple_gate[0]), wpp, row(final_norm), tm)

    def heads(a, b, t):
        return a.reshape(1, b, t, N_HEADS, HEAD_DIM)

    b, t, _ = x_prompt.shape
    tm = _row_tile(t, 512)
    h1, (q, k, v, kd, vd, kf, vf, lf, qaug, kaug) = dense_pre(x_prompt, tm)
    tq = _row_tile(t, 512)
    rg = min(tq, ATTN_ROW_GROUP)
    tk = min(tq, ATTN_KV_CHUNK)
    q_tab, k_tab = _alibi_tables(t, tq)
    o_d = _attend_prompt(q, k, v, q_tab[None], k_tab[None], slopes, lam, subln_pair, is_diff=True, tq=tq, tk=tk, rg=rg)
    o_f = _attend_prompt(q, k, v, qaug, kaug, slopes, lam, subln_pair, is_diff=False, tq=tq, tk=tk, rg=rg)
    o = jnp.concatenate([o_d, o_f], axis=-1).reshape(b * t, 2 * w)
    y_prompt = dense_post(h1, o, p_prompt[0].reshape(b * t, -1), tm).reshape(b, t, d)
    prompt_new = (heads(kd, b, t), heads(vd, b, t), heads(kf, b, t), heads(vf, b, t), lf.reshape(1, b, t, N_HEADS))

    nb, ts, _ = x_sample.shape
    assert ts == 1, "decode kernel handles one query per row"
    xs = x_sample.reshape(1, nb, d)
    h1s, (qs, _, _, kds, vds, kfs, vfs, lfs, _, _) = dense_pre(xs, nb)
    flat = lambda a: a.reshape(nb, 1, w)
    kv_view = lambda c: jnp.transpose(c[0], (0, 2, 3, 1)).reshape(c.shape[1], w, c.shape[2])
    caches = (kv_view(cache_diff_k), kv_view(cache_diff_v), kv_view(cache_fox_k), kv_view(cache_fox_v),
              jnp.transpose(cache_fox_logf[0], (0, 2, 1)))
    o_ds, o_fs = _attend_decode(flat(qs[0, :, :w]), flat(qs[0, :, w:]), flat(kds), flat(vds), flat(kfs), flat(vfs),
                                lfs.reshape(nb, N_HEADS, 1), caches, page_table, slopes.reshape(N_HEADS, 1),
                                lam, jnp.tile(diff_subln[0], N_HEADS).reshape(1, w),
                                pages_per_step=DECODE_PAGES_PER_STEP)
    o_s = bf(jnp.concatenate([o_ds.reshape(nb, w), o_fs.reshape(nb, w)], axis=-1))
    y_sample = dense_post(h1s, o_s, p_sample[0].reshape(nb, -1), nb).reshape(nb, 1, d)
    sample_new = (heads(kds, nb, 1), heads(vds, nb, 1), heads(kfs, nb, 1), heads(vfs, nb, 1),
                  lfs.reshape(1, nb, 1, N_HEADS))

    return (y_prompt, y_sample) + prompt_new + sample_new
```

```python
import functools
import math

import numpy as np
import jax
import jax.numpy as jnp
from jax import lax
from jax.experimental import pallas as pl
from jax.experimental.pallas import tpu as pltpu

F32 = jnp.float32
BF16 = jnp.bfloat16

HEAD_DIM = 64
N_HEADS = 8
HEADS_W = N_HEADS * HEAD_DIM
DIFF_QK = HEAD_DIM // 2
LANES = 128
N_PAIRS = HEADS_W // LANES
AUG_PER_HEAD = 6
RMS_EPS = 1e-6
LOG2E = 1.4426950408889634
LAMBDA_INIT = 0.8 - 0.6 * math.exp(-0.3 * 0)
NEG_INF = float("-inf")
VMEM_LIMIT = 56 * 1024 * 1024
DECODE_PAGES_PER_STEP = 16
ATTN_ROWS_PER_STEP = 2048
ATTN_ROW_GROUP = 256
ATTN_KV_CHUNK = 512
ATTN_CHAINS_PER_TRIP = 32


def _rms(x, g):
    return x * lax.rsqrt(jnp.mean(x * x, axis=-1, keepdims=True) + RMS_EPS) * g


def _dot(a, b):
    return jnp.dot(a, b, preferred_element_type=F32)


def _dot_nt(a, b):
    return lax.dot_general(a, b, (((1,), (1,)), ((), ())), preferred_element_type=F32)


def _split3(x):
    hi = x.astype(BF16).astype(F32)
    r = x - hi
    mid = r.astype(BF16).astype(F32)
    lo = (r - mid).astype(BF16).astype(F32)
    return hi, mid, lo


def _const_spec(shape):
    nd = len(shape)
    return pl.BlockSpec(shape, lambda *_: (0,) * nd, pipeline_mode=pl.Buffered(1))


def _swiglu(hn, wg_ref, wu_ref, wd_ref, chunk):
    d_ff = wg_ref.shape[1]
    out = None
    for c0 in range(0, d_ff, chunk):
        g = _dot(hn, wg_ref[:, c0:c0 + chunk])
        u = _dot(hn, wu_ref[:, c0:c0 + chunk])
        a = (g * jax.nn.sigmoid(g) * u).astype(BF16)
        part = _dot(a, wd_ref[c0:c0 + chunk, :])
        out = part if out is None else out + part
    return out


def _ff_chunk(d_ff):
    return d_ff


def _ffn_kernel(x_ref, g_ref, wg_ref, wu_ref, wd_ref, h_ref, *, chunk):
    x = x_ref[...]
    hn = _rms(x, g_ref[...]).astype(BF16)
    h_ref[...] = x + 0.5 * _swiglu(hn, wg_ref, wu_ref, wd_ref, chunk)


def _ffn(x, g, wg, wu, wd, tm):
    n, d = x.shape
    d_ff = wg.shape[1]
    return pl.pallas_call(
        functools.partial(_ffn_kernel, chunk=_ff_chunk(d_ff)),
        grid=(n // tm,),
        in_specs=[pl.BlockSpec((tm, d), lambda i: (i, 0)),
                  _const_spec((1, d)), _const_spec((d, d_ff)), _const_spec((d, d_ff)), _const_spec((d_ff, d))],
        out_specs=pl.BlockSpec((tm, d), lambda i: (i, 0)),
        out_shape=jax.ShapeDtypeStruct((n, d), F32),
        compiler_params=pltpu.CompilerParams(dimension_semantics=("arbitrary",), vmem_limit_bytes=VMEM_LIMIT),
        name="ffn1",
    )(x, g, wg, wu, wd)


def _proj_kernel(h_ref, g_ref, win_ref, wf_ref, bf_ref, tri_ref, selq_ref, selk_ref,
                 q_ref, k_ref, v_ref, kd_ref, vd_ref, kf_ref, vf_ref, lf_ref, qaug_ref, kaug_ref,
                 carry_ref, *, q_scale_d, q_scale_f):
    t = pl.program_id(1)
    w = HEADS_W

    @pl.when(t == 0)
    def _():
        carry_ref[...] = jnp.zeros_like(carry_ref)

    hn = _rms(h_ref[0], g_ref[...]).astype(BF16)
    z = _dot(hn, win_ref[...])
    kd, vd = z[:, w:2 * w], z[:, 2 * w:3 * w]
    kf, vf = z[:, 4 * w:5 * w], z[:, 5 * w:6 * w]
    kd_ref[0], vd_ref[0], kf_ref[0], vf_ref[0] = kd, vd, kf, vf
    q_ref[0, :, :w] = (z[:, :w] * q_scale_d).astype(BF16)
    q_ref[0, :, w:] = (z[:, 3 * w:4 * w] * q_scale_f).astype(BF16)
    k_ref[0, :, :w] = kd.astype(BF16)
    k_ref[0, :, w:] = kf.astype(BF16)
    v_ref[0, :, :w] = vd.astype(BF16)
    v_ref[0, :, w:] = vf.astype(BF16)

    x = _dot(hn, wf_ref[...]) + bf_ref[...]
    lane = lax.broadcasted_iota(jnp.int32, x.shape, 1)
    lf = -(jnp.maximum(-x, 0.0) + jnp.log1p(jnp.exp(-jnp.abs(x))))
    lf = jnp.where(lane < N_HEADS, lf, 0.0)
    lf_ref[0] = lf[:, :N_HEADS]

    hi, mid, lo = _split3(lf * LOG2E)
    tri = tri_ref[...]
    c = (_dot(tri, hi.astype(BF16)) + _dot(tri, mid.astype(BF16)) + _dot(tri, lo.astype(BF16))
         + carry_ref[...])
    tm = c.shape[0]
    carry_ref[...] = c[tm - 1:tm, :]

    chi, cmid, clo = _split3(c)
    packed = (chi + pltpu.roll(cmid, N_HEADS, 1) + pltpu.roll(clo, 2 * N_HEADS, 1)
              + jnp.where(lane == 3 * N_HEADS, 1.0, 0.0)).astype(BF16)
    qaug_ref[0] = _dot(packed, selq_ref[...]).astype(BF16)
    kaug_ref[0] = _dot(packed, selk_ref[...]).astype(BF16)


def _aug_select_matrices():
    selq = np.zeros((LANES, N_PAIRS * LANES), np.float32)
    selk = np.zeros((LANES, N_PAIRS * LANES), np.float32)
    one = 3 * N_HEADS
    for p in range(N_PAIRS):
        for s in range(2):
            head = 2 * p + s
            base = p * LANES + s * AUG_PER_HEAD
            for piece in range(3):
                selq[piece * N_HEADS + head, base + piece] = 1.0
                selq[one, base + 3 + piece] = 1.0
                selk[one, base + piece] = 1.0
                selk[piece * N_HEADS + head, base + 3 + piece] = -1.0
    return jnp.asarray(selq, BF16), jnp.asarray(selk, BF16)


def _project(h, g, w_in, w_f, b_f, tm):
    b, t, d = h.shape
    w = HEADS_W
    selq, selk = _aug_select_matrices()
    tri = jnp.asarray(np.tril(np.ones((tm, tm), np.float32)), BF16)
    row = lambda width: pl.BlockSpec((1, tm, width), lambda i, j: (i, j, 0))
    sds = lambda width, dt: jax.ShapeDtypeStruct((b, t, width), dt)
    kern = functools.partial(_proj_kernel,
                             q_scale_d=DIFF_QK ** -0.5 * LOG2E, q_scale_f=HEAD_DIM ** -0.5 * LOG2E)
    return pl.pallas_call(
        kern,
        grid=(b, t // tm),
        in_specs=[row(d), _const_spec((1, d)), _const_spec(w_in.shape), _const_spec(w_f.shape),
                  _const_spec((1, LANES)), _const_spec((tm, tm)), _const_spec(selq.shape), _const_spec(selk.shape)],
        out_specs=[row(2 * w), row(2 * w), row(2 * w), row(w), row(w), row(w), row(w),
                   row(N_HEADS), row(N_PAIRS * LANES), row(N_PAIRS * LANES)],
        out_shape=[sds(2 * w, BF16), sds(2 * w, BF16), sds(2 * w, BF16),
                   sds(w, F32), sds(w, F32), sds(w, F32), sds(w, F32),
                   sds(N_HEADS, F32), sds(N_PAIRS * LANES, BF16), sds(N_PAIRS * LANES, BF16)],
        scratch_shapes=[pltpu.VMEM((1, LANES), F32)],
        compiler_params=pltpu.CompilerParams(dimension_semantics=("arbitrary", "arbitrary"),
                                             vmem_limit_bytes=VMEM_LIMIT),
        name="project",
    )(h, g, w_in, w_f, b_f, tri, selq, selk)


def _alibi_kernel(q_ref, k_ref):
    rows = q_ref.shape[0]
    pos = (pl.program_id(0) * rows + lax.broadcasted_iota(jnp.int32, (rows, LANES), 0)).astype(F32)
    lane = lax.broadcasted_iota(jnp.int32, (rows, LANES), 1)
    hi, mid, lo = _split3(pos * LOG2E)
    pieces = jnp.where(lane % 3 == 0, hi, jnp.where(lane % 3 == 1, mid, lo))
    q_ref[...] = jnp.where(lane < 3, -pieces, jnp.where(lane < 6, 1.0, 0.0)).astype(BF16)
    k_ref[...] = jnp.where(lane < 3, 1.0, jnp.where(lane < 6, pieces, 0.0)).astype(BF16)


def _alibi_tables(n, rows):
    return pl.pallas_call(
        _alibi_kernel,
        grid=(n // rows,),
        out_specs=[pl.BlockSpec((rows, LANES), lambda i: (i, 0))] * 2,
        out_shape=[jax.ShapeDtypeStruct((n, LANES), BF16)] * 2,
        name="alibi_tables",
    )()


def _alibi_slopes():
    slopes = 2.0 ** (-8.0 * np.arange(1, N_HEADS + 1, dtype=np.float64) / N_HEADS)
    assert np.all(np.log2(slopes) == np.round(np.log2(slopes))), "bias columns assume power-of-two slopes"
    return slopes.astype(np.float32)


def _lambda_full(lam_ref):
    l = lam_ref[...]
    s1 = jnp.sum(l[0:1] * l[1:2], axis=-1, keepdims=True)
    s2 = jnp.sum(l[2:3] * l[3:4], axis=-1, keepdims=True)
    return jnp.exp(s1) - jnp.exp(s2) + LAMBDA_INIT


def _head_rms(o, subln_row, lane):
    sq = o * o
    first = lane < HEAD_DIM
    sum_a = jnp.sum(jnp.where(first, sq, 0.0), axis=-1, keepdims=True)
    sum_b = jnp.sum(jnp.where(first, 0.0, sq), axis=-1, keepdims=True)
    ms = jnp.where(first, sum_a, sum_b) * (1.0 / HEAD_DIM)
    return o * lax.rsqrt(ms + RMS_EPS) * subln_row


def _attn_kernel(slope_ref, q_ref, k_ref, v_ref, qaug_ref, kaug_ref, lam_ref, subln_ref, o_ref,
                 qs_ref, m_ref, acc_ref, *, is_diff, tq, tk, rg, unroll):
    n_maps = 4 if is_diff else 2
    rows = n_maps * tq
    pair = pl.program_id(1)
    qi = pl.program_id(2)
    lane = lax.broadcasted_iota(jnp.int32, (tq, LANES), 1)

    q = q_ref[0].astype(F32)
    qa = qaug_ref[0].astype(F32)
    map_w = LANES // n_maps
    for m in range(n_maps):
        qm = jnp.where((lane >= m * map_w) & (lane < (m + 1) * map_w), q, 0.0)
        if is_diff:
            aug = qa * slope_ref[2 * pair + m // 2]
        else:
            aug = jnp.where((lane >= m * AUG_PER_HEAD) & (lane < (m + 1) * AUG_PER_HEAD), qa, 0.0)
        qs_ref[m * tq:(m + 1) * tq, :LANES] = qm.astype(BF16)
        qs_ref[m * tq:(m + 1) * tq, LANES:] = aug.astype(BF16)

    m_ref[...] = jnp.full_like(m_ref, NEG_INF)
    acc_ref[...] = jnp.zeros_like(acc_ref)
    ones = jnp.ones((tk, LANES), BF16)

    def step(j, diag=None):
        start = pl.multiple_of(j * tk, tk)
        kfull = jnp.concatenate([k_ref[0, pl.ds(start, tk), :], kaug_ref[0, pl.ds(start, tk), :]], axis=1)
        vfull = jnp.concatenate([v_ref[0, pl.ds(start, tk), :], ones], axis=1)
        for r0 in range(0, rows, rg):
            rs = slice(r0, r0 + rg)
            kw = tk
            if diag is not None:
                kw = max(0, min(tk, r0 % tq + rg - diag * tk))
                if kw == 0:
                    continue
            s = _dot_nt(qs_ref[rs, :], kfull[:kw])
            if diag is not None and diag * tk + kw - 1 > r0 % tq:
                r = lax.broadcasted_iota(jnp.int32, (rg, kw), 0) + (r0 % tq + qi * tq)
                c = lax.broadcasted_iota(jnp.int32, (rg, kw), 1) + start
                s = jnp.where(c <= r, s, NEG_INF)
            m_prev = m_ref[rs, :]
            m_new = jnp.maximum(m_prev, jnp.max(s, axis=1, keepdims=True))
            p = jnp.exp2((s - jnp.concatenate([m_new] * (kw // LANES), axis=1)).astype(BF16))
            alpha = jnp.exp2(m_prev - m_new)
            acc_ref[rs, :] = acc_ref[rs, :] * jnp.concatenate([alpha, alpha], axis=1) + _dot(p, vfull[:kw])
            m_ref[rs, :] = m_new

    n_full = (qi * tq) // tk

    def body(jj, carry):
        for u in range(unroll):
            step(unroll * jj + u)
        return carry

    lax.fori_loop(0, n_full // unroll, body, 0)

    done = (n_full // unroll) * unroll
    part = unroll // 2
    while part >= 1:
        @pl.when((n_full - done) % (2 * part) >= part)
        def _(done=done, part=part):
            for u in range(part):
                step(done + u)
        done = done + jnp.where((n_full - done) % (2 * part) >= part, part, 0)
        part //= 2

    for d in range(tq // tk):
        step(n_full + d, diag=d)

    def normalized(m):
        a = acc_ref[m * tq:(m + 1) * tq, :]
        return a[:, :LANES] / a[:, LANES:]

    first = lane < HEAD_DIM
    if is_diff:
        lam = _lambda_full(lam_ref)
        o = jnp.where(first, normalized(0) - lam * normalized(1), normalized(2) - lam * normalized(3))
        o = _head_rms(o, subln_ref[...], lane) * (1.0 - LAMBDA_INIT)
    else:
        o = jnp.where(first, normalized(0), normalized(1))
    o_ref[0] = o.astype(o_ref.dtype)


def _attend_prompt(q, k, v, qaug, kaug, slopes, lam, subln, *, is_diff, tq, tk, rg):
    b, t, _ = q.shape
    n_maps = 4 if is_diff else 2
    col0 = 0 if is_diff else N_PAIRS
    rows = n_maps * tq
    assert tq % tk == 0 and tq % rg == 0
    unroll = max(1, ATTN_CHAINS_PER_TRIP // (rows // rg))
    assert unroll & (unroll - 1) == 0
    if is_diff:
        qaug_spec = pl.BlockSpec((1, tq, LANES), lambda i, p, j, s: (0, j, 0))
        kaug_spec = pl.BlockSpec((1, t, LANES), lambda i, p, j, s: (0, 0, 0))
    else:
        qaug_spec = pl.BlockSpec((1, tq, LANES), lambda i, p, j, s: (i, j, p))
        kaug_spec = pl.BlockSpec((1, t, LANES), lambda i, p, j, s: (i, 0, p))
    grid_spec = pltpu.PrefetchScalarGridSpec(
        num_scalar_prefetch=1,
        grid=(b, N_PAIRS, t // tq),
        in_specs=[pl.BlockSpec((1, tq, LANES), lambda i, p, j, s: (i, j, col0 + p)),
                  pl.BlockSpec((1, t, LANES), lambda i, p, j, s: (i, 0, col0 + p)),
                  pl.BlockSpec((1, t, LANES), lambda i, p, j, s: (i, 0, col0 + p)),
                  qaug_spec, kaug_spec,
                  pl.BlockSpec(lam.shape, lambda i, p, j, s: (0, 0)),
                  pl.BlockSpec(subln.shape, lambda i, p, j, s: (0, 0))],
        out_specs=pl.BlockSpec((1, tq, LANES), lambda i, p, j, s: (i, j, p)),
        scratch_shapes=[pltpu.VMEM((rows, 2 * LANES), BF16),
                        pltpu.VMEM((rows, LANES), F32),
                        pltpu.VMEM((rows, 2 * LANES), F32)],
    )
    return pl.pallas_call(
        functools.partial(_attn_kernel, is_diff=is_diff, tq=tq, tk=tk, rg=rg, unroll=unroll),
        grid_spec=grid_spec,
        out_shape=jax.ShapeDtypeStruct((b, t, HEADS_W), BF16),
        compiler_params=pltpu.CompilerParams(dimension_semantics=("arbitrary",) * 3,
                                             vmem_limit_bytes=VMEM_LIMIT),
        name="attn_diff" if is_diff else "attn_fox",
    )(slopes, q, k, v, qaug, kaug, lam, subln)


def _decode_kernel(pt_ref, qd_ref, qf_ref, knd_ref, vnd_ref, knf_ref, vnf_ref, lfn_ref, slope_ref,
                   lam_ref, subln_ref, usuf_ref, *rest, pages_per_step, n_pages, page):
    g = pages_per_step
    dk_refs, dv_refs = rest[0:g], rest[g:2 * g]
    fk_refs, fv_refs, lf_refs = rest[2 * g:3 * g], rest[3 * g:4 * g], rest[4 * g:5 * g]
    od_ref, of_ref = rest[5 * g], rest[5 * g + 1]
    qbd_ref, qbf_ref, md_ref, ld_ref, accd_ref, mf_ref, lsum_ref, accf_ref, tail_ref = rest[5 * g + 2:]
    del pt_ref
    j = pl.program_id(1)
    nh = N_HEADS
    w = HEADS_W
    sub = lax.broadcasted_iota(jnp.int32, (2 * nh, w), 0)
    lane = lax.broadcasted_iota(jnp.int32, (2 * nh, w), 1)
    own = (lane // HEAD_DIM) == (sub % nh)
    own8 = own[:nh]

    @pl.when(j == 0)
    def _():
        half = ((lane % HEAD_DIM) < DIFF_QK) == (sub < nh)
        qbd = jnp.where(own & half, jnp.broadcast_to(qd_ref[...].astype(F32), (2 * nh, w)), 0.0)
        qbf = jnp.where(own8, jnp.broadcast_to(qf_ref[...].astype(F32), (nh, w)), 0.0)
        qbd_ref[...] = qbd.astype(BF16)
        qbf_ref[...] = qbf.astype(BF16)
        md_ref[...] = jnp.sum(qbd * knd_ref[...], axis=-1, keepdims=True)
        ld_ref[...] = jnp.ones_like(ld_ref)
        accd_ref[...] = jnp.broadcast_to(vnd_ref[...], (2 * nh, w))
        mf_ref[...] = jnp.sum(qbf * knf_ref[...], axis=-1, keepdims=True)
        lsum_ref[...] = jnp.ones_like(lsum_ref)
        accf_ref[...] = jnp.broadcast_to(vnf_ref[...], (nh, w))
        tail_ref[...] = jnp.zeros_like(tail_ref)

    past = n_pages * page
    pos_in_page = lax.broadcasted_iota(jnp.int32, (2 * nh, page), 1)
    slope2 = jnp.concatenate([slope_ref[...], slope_ref[...]], axis=0) * LOG2E
    lf_new = lfn_ref[...] * LOG2E
    usuf = usuf_ref[...]
    tail = tail_ref[...]
    sd, sf = [], []
    for i in range(g):
        logical = n_pages - 1 - (j * g + i)
        s = _dot(qbd_ref[...], dk_refs[i][...].astype(BF16))
        dist = (past - (logical * page + pos_in_page)).astype(F32)
        sd.append(s - slope2 * dist)
        s = _dot(qbf_ref[...], fk_refs[i][...].astype(BF16))
        hi, mid, lo = _split3(lf_refs[i][...] * LOG2E)
        suf = _dot(hi.astype(BF16), usuf) + _dot(mid.astype(BF16), usuf) + _dot(lo.astype(BF16), usuf)
        sf.append(s + suf[:, :page] + tail + lf_new)
        tail = tail + suf[:, page:]
    tail_ref[...] = tail

    def online(parts, m_ref, l_ref, acc_ref, v_refs):
        s = jnp.concatenate(parts, axis=1)
        m_prev = m_ref[...]
        m_new = jnp.maximum(m_prev, jnp.max(s, axis=-1, keepdims=True))
        p = jnp.exp2(s - m_new)
        alpha = jnp.exp2(m_prev - m_new)
        l_ref[...] = alpha * l_ref[...] + jnp.sum(p, axis=-1, keepdims=True)
        pv = None
        for i in range(g):
            part = _dot_nt(p[:, i * page:(i + 1) * page].astype(BF16), v_refs[i][...].astype(BF16))
            pv = part if pv is None else pv + part
        acc_ref[...] = alpha * acc_ref[...] + pv
        m_ref[...] = m_new

    online(sd, md_ref, ld_ref, accd_ref, dv_refs)
    online(sf, mf_ref, lsum_ref, accf_ref, fv_refs)

    @pl.when(j == pl.num_programs(1) - 1)
    def _():
        o_f = jnp.where(own8, accf_ref[...] / lsum_ref[...], 0.0)
        of_ref[...] = jnp.sum(o_f, axis=0, keepdims=True)
        lam = _lambda_full(lam_ref)
        a = accd_ref[...] / ld_ref[...]
        o = jnp.where(own8, a[:nh] - lam * a[nh:], 0.0)
        ms = jnp.sum(o * o, axis=-1, keepdims=True) * (1.0 / HEAD_DIM)
        o = o * lax.rsqrt(ms + RMS_EPS)
        od_ref[...] = jnp.sum(o, axis=0, keepdims=True) * subln_ref[...] * (1.0 - LAMBDA_INIT)


def _suffix_matrix(page):
    u = np.ones((page, 2 * page), np.float32)
    u[:, :page] = np.arange(page)[:, None] > np.arange(page)[None, :]
    return jnp.asarray(u, BF16)


def _attend_decode(qd, qf, knd, vnd, knf, vnf, lf_new, caches, page_table, slopes, lam, subln, pages_per_step):
    cdk, cdv, cfk, cfv, clf = caches
    nb, n_pages = page_table.shape
    page = cdk.shape[-1]
    assert page == LANES
    g = min(pages_per_step, n_pages)
    assert n_pages % g == 0
    usuf = _suffix_matrix(page)
    nh = N_HEADS
    w = HEADS_W

    def page_spec(i, rows):
        return pl.BlockSpec((None, rows, page),
                            lambda b, j, pt: (pt[b * n_pages + n_pages - 1 - (j * g + i)], 0, 0))

    per_b = lambda shape: pl.BlockSpec((None,) + shape, lambda b, j, pt: (b,) + (0,) * len(shape))
    const = lambda a: pl.BlockSpec(a.shape, lambda b, j, pt: (0,) * a.ndim)
    in_specs = ([per_b((1, w))] * 6 + [per_b((nh, 1)), const(slopes), const(lam), const(subln), const(usuf)]
                + [page_spec(i, w) for _ in range(4) for i in range(g)]
                + [page_spec(i, nh) for i in range(g)])
    scratch = [pltpu.VMEM((2 * nh, w), BF16), pltpu.VMEM((nh, w), BF16),
               pltpu.VMEM((2 * nh, 1), F32), pltpu.VMEM((2 * nh, 1), F32), pltpu.VMEM((2 * nh, w), F32),
               pltpu.VMEM((nh, 1), F32), pltpu.VMEM((nh, 1), F32), pltpu.VMEM((nh, w), F32),
               pltpu.VMEM((nh, page), F32)]
    grid_spec = pltpu.PrefetchScalarGridSpec(
        num_scalar_prefetch=1, grid=(nb, n_pages // g), in_specs=in_specs,
        out_specs=[per_b((1, w))] * 2, scratch_shapes=scratch)
    return pl.pallas_call(
        functools.partial(_decode_kernel, pages_per_step=g, n_pages=n_pages, page=page),
        grid_spec=grid_spec,
        out_shape=[jax.ShapeDtypeStruct((nb, 1, w), F32)] * 2,
        compiler_params=pltpu.CompilerParams(dimension_semantics=("arbitrary", "arbitrary"),
                                             vmem_limit_bytes=VMEM_LIMIT),
        name="attn_decode",
    )(page_table.reshape(-1), qd, qf, knd, vnd, knf, vnf, lf_new, slopes, lam, subln, usuf,
      *([cdk] * g + [cdv] * g + [cfk] * g + [cfv] * g + [clf] * g))


def _post_kernel(h_ref, od_ref, of_ref, p_ref, wo_ref, g2_ref, wg_ref, wu_ref, wd_ref, gp_ref, wpg_ref, bpg_ref,
                 wpp_ref, gf_ref, y_ref, *, chunk):
    w = od_ref.shape[1]
    h = h_ref[...] + _dot(od_ref[...], wo_ref[:w, :]) + _dot(of_ref[...], wo_ref[w:, :])
    hn = _rms(h, g2_ref[...]).astype(BF16)
    h = h + 0.5 * _swiglu(hn, wg_ref, wu_ref, wd_ref, chunk)
    hn = _rms(h, gp_ref[...]).astype(BF16)
    gate = jax.nn.sigmoid(_dot(hn, wpg_ref[...]) + bpg_ref[...])
    h = h + gate * _dot(p_ref[...].astype(BF16), wpp_ref[...])
    y_ref[...] = _rms(h, gf_ref[...])


def _post(h, o_d, o_f, p, wo, g2, wg, wu, wd, gp, wpg, bpg, wpp, gf, tm):
    n, d = h.shape
    row = lambda width: pl.BlockSpec((tm, width), lambda i: (i, 0))
    consts = (wo, g2, wg, wu, wd, gp, wpg, bpg, wpp, gf)
    return pl.pallas_call(
        functools.partial(_post_kernel, chunk=_ff_chunk(wg.shape[1])),
        grid=(n // tm,),
        in_specs=([row(d), row(o_d.shape[1]), row(o_f.shape[1]), row(p.shape[1])]
                  + [_const_spec(c.shape) for c in consts]),
        out_specs=row(d),
        out_shape=jax.ShapeDtypeStruct((n, d), F32),
        compiler_params=pltpu.CompilerParams(dimension_semantics=("arbitrary",), vmem_limit_bytes=VMEM_LIMIT),
        name="post",
    )(h, o_d, o_f, p, *consts)


def _row_tile(n, target):
    tm = min(n, target)
    assert n % tm == 0 and (tm % 8 == 0 or tm == n), (n, tm)
    return tm


def kernel(x_prompt, x_sample, cache_diff_k, cache_diff_v, cache_fox_k, cache_fox_v, cache_fox_logf, page_table, p_prompt, p_sample, ffn1_norm, ffn1_w_gate, ffn1_w_up, ffn1_w_down, mix_norm, w_in, b_forget, lambda_q1, lambda_k1, lambda_q2, lambda_k2, diff_subln, w_out, ffn2_norm, ffn2_w_gate, ffn2_w_up, ffn2_w_down, ple_norm, w_ple_gate, b_ple_gate, w_ple_proj, final_norm):
    depth = w_in.shape[0]
    assert depth == 1, "single-layer step only"
    d = x_prompt.shape[-1]
    w = HEADS_W
    assert w_in.shape[-1] == 6 * w + N_HEADS and w_out.shape[1:] == (2 * w, d)
    bf = lambda a: a.astype(BF16)
    row = lambda a: a.reshape(1, -1)

    wg1, wu1, wd1 = bf(ffn1_w_gate[0]), bf(ffn1_w_up[0]), bf(ffn1_w_down[0])
    wg2, wu2, wd2 = bf(ffn2_w_gate[0]), bf(ffn2_w_up[0]), bf(ffn2_w_down[0])
    w_qkv = bf(w_in[0, :, :6 * w])
    w_f = jnp.pad(bf(w_in[0, :, 6 * w:]), ((0, 0), (0, LANES - N_HEADS)))
    b_f = jnp.pad(b_forget[0], (0, LANES - N_HEADS)).reshape(1, LANES)
    wo, wpg, wpp = bf(w_out[0]), bf(w_ple_gate[0]), bf(w_ple_proj[0])
    lam = jnp.stack([lambda_q1[0], lambda_k1[0], lambda_q2[0], lambda_k2[0]])
    subln_pair = jnp.tile(diff_subln[0], 2).reshape(1, LANES)
    slopes = jnp.asarray(_alibi_slopes())

    def dense_pre(x3, tm):
        b, t, _ = x3.shape
        h1 = _ffn(x3.reshape(b * t, d), row(ffn1_norm[0]), wg1, wu1, wd1, tm)
        outs = _project(h1.reshape(b, t, d), row(mix_norm[0]), w_qkv, w_f, b_f, tm)
        return h1, outs

    def dense_post(h1, o_diff, o_fox, p2, tm):
        return _post(h1, o_diff, o_fox, p2, wo, row(ffn2_norm[0]), wg2, wu2, wd2, row(ple_norm[0]), wpg,
                     row(b_ple_gate[0]), wpp, row(final_norm), tm)

    def heads(a, b, t):
        return a.reshape(1, b, t, N_HEADS, HEAD_DIM)

    b, t, _ = x_prompt.shape
    tm = _row_tile(t, 512)
    h1, (q, k, v, kd, vd, kf, vf, lf, qaug, kaug) = dense_pre(x_prompt, tm)
    tq_d, tq_f = _row_tile(t, ATTN_ROWS_PER_STEP // 4), _row_tile(t, ATTN_ROWS_PER_STEP // 2)
    rg = min(tq_d, ATTN_ROW_GROUP)
    tk = min(tq_d, ATTN_KV_CHUNK)
    q_tab, k_tab = _alibi_tables(t, tq_d)
    o_d = _attend_prompt(q, k, v, q_tab[None], k_tab[None], slopes, lam, subln_pair, is_diff=True, tq=tq_d, tk=tk, rg=rg)
    o_f = _attend_prompt(q, k, v, qaug, kaug, slopes, lam, subln_pair, is_diff=False, tq=tq_f, tk=tk, rg=rg)
    y_prompt = dense_post(h1, o_d.reshape(b * t, w), o_f.reshape(b * t, w), p_prompt[0].reshape(b * t, -1),
                          tm).reshape(b, t, d)
    prompt_new = (heads(kd, b, t), heads(vd, b, t), heads(kf, b, t), heads(vf, b, t), lf.reshape(1, b, t, N_HEADS))

    nb, ts, _ = x_sample.shape
    assert ts == 1, "decode kernel handles one query per row"
    xs = x_sample.reshape(1, nb, d)
    h1s, (qs, _, _, kds, vds, kfs, vfs, lfs, _, _) = dense_pre(xs, nb)
    flat = lambda a: a.reshape(nb, 1, w)
    kv_view = lambda c: jnp.transpose(c[0], (0, 2, 3, 1)).reshape(c.shape[1], w, c.shape[2])
    caches = (kv_view(cache_diff_k), kv_view(cache_diff_v), kv_view(cache_fox_k), kv_view(cache_fox_v),
              jnp.transpose(cache_fox_logf[0], (0, 2, 1)))
    o_ds, o_fs = _attend_decode(flat(qs[0, :, :w]), flat(qs[0, :, w:]), flat(kds), flat(vds), flat(kfs), flat(vfs),
                                lfs.reshape(nb, N_HEADS, 1), caches, page_table, slopes.reshape(N_HEADS, 1),
                                lam, jnp.tile(diff_subln[0], N_HEADS).reshape(1, w),
                                pages_per_step=DECODE_PAGES_PER_STEP)
    y_sample = dense_post(h1s, bf(o_ds.reshape(nb, w)), bf(o_fs.reshape(nb, w)),
                          p_sample[0].reshape(nb, -1), nb).reshape(nb, 1, d)
    sample_new = (heads(kds, nb, 1), heads(vds, nb, 1), heads(kfs, nb, 1), heads(vfs, nb, 1),
                  lfs.reshape(1, nb, 1, N_HEADS))

    return (y_prompt, y_sample) + prompt_new + sample_new
```

```python
import functools
import math

import numpy as np
import jax
import jax.numpy as jnp
from jax import lax
from jax.experimental import pallas as pl
from jax.experimental.pallas import tpu as pltpu

F32 = jnp.float32
BF16 = jnp.bfloat16

HEAD_DIM = 64
N_HEADS = 8
HEADS_W = N_HEADS * HEAD_DIM
DIFF_QK = HEAD_DIM // 2
LANES = 128
N_PAIRS = HEADS_W // LANES
AUG_PER_HEAD = 6
RMS_EPS = 1e-6
LOG2E = 1.4426950408889634
LAMBDA_INIT = 0.8 - 0.6 * math.exp(-0.3 * 0)
NEG_INF = float("-inf")
VMEM_LIMIT = 56 * 1024 * 1024
DECODE_PAGES_PER_STEP = 16
ATTN_ROWS_PER_STEP = 2048
ATTN_ROW_GROUP = 256
ATTN_KV_CHUNK = 512
ATTN_CHAINS_PER_TRIP = 32


def _rms(x, g):
    return x * lax.rsqrt(jnp.mean(x * x, axis=-1, keepdims=True) + RMS_EPS) * g


def _dot(a, b):
    return jnp.dot(a, b, preferred_element_type=F32)


def _dot_nt(a, b):
    return lax.dot_general(a, b, (((1,), (1,)), ((), ())), preferred_element_type=F32)


def _split3(x):
    hi = x.astype(BF16).astype(F32)
    r = x - hi
    mid = r.astype(BF16).astype(F32)
    lo = (r - mid).astype(BF16).astype(F32)
    return hi, mid, lo


def _const_spec(shape):
    nd = len(shape)
    return pl.BlockSpec(shape, lambda *_: (0,) * nd, pipeline_mode=pl.Buffered(1))


def _swiglu(hn, wg_ref, wu_ref, wd_ref, chunk):
    d_ff = wg_ref.shape[1]
    out = None
    for c0 in range(0, d_ff, chunk):
        g = _dot(hn, wg_ref[:, c0:c0 + chunk])
        u = _dot(hn, wu_ref[:, c0:c0 + chunk])
        a = (g * jax.nn.sigmoid(g) * u).astype(BF16)
        part = _dot(a, wd_ref[c0:c0 + chunk, :])
        out = part if out is None else out + part
    return out


def _ff_chunk(d_ff):
    return d_ff


def _ffn_kernel(x_ref, g_ref, wg_ref, wu_ref, wd_ref, h_ref, *, chunk):
    x = x_ref[...]
    hn = _rms(x, g_ref[...]).astype(BF16)
    h_ref[...] = x + 0.5 * _swiglu(hn, wg_ref, wu_ref, wd_ref, chunk)


def _ffn(x, g, wg, wu, wd, tm):
    n, d = x.shape
    d_ff = wg.shape[1]
    return pl.pallas_call(
        functools.partial(_ffn_kernel, chunk=_ff_chunk(d_ff)),
        grid=(n // tm,),
        in_specs=[pl.BlockSpec((tm, d), lambda i: (i, 0)),
                  _const_spec((1, d)), _const_spec((d, d_ff)), _const_spec((d, d_ff)), _const_spec((d_ff, d))],
        out_specs=pl.BlockSpec((tm, d), lambda i: (i, 0)),
        out_shape=jax.ShapeDtypeStruct((n, d), F32),
        compiler_params=pltpu.CompilerParams(dimension_semantics=("arbitrary",), vmem_limit_bytes=VMEM_LIMIT),
        name="ffn1",
    )(x, g, wg, wu, wd)


def _proj_kernel(h_ref, g_ref, win_ref, wf_ref, bf_ref, tri_ref, selq_ref, selk_ref,
                 q_ref, k_ref, v_ref, kd_ref, vd_ref, kf_ref, vf_ref, lf_ref, qaug_ref, kaug_ref,
                 carry_ref, *, q_scale_d, q_scale_f):
    t = pl.program_id(1)
    w = HEADS_W

    @pl.when(t == 0)
    def _():
        carry_ref[...] = jnp.zeros_like(carry_ref)

    hn = _rms(h_ref[0], g_ref[...]).astype(BF16)
    z = _dot(hn, win_ref[...])
    kd, vd = z[:, w:2 * w], z[:, 2 * w:3 * w]
    kf, vf = z[:, 4 * w:5 * w], z[:, 5 * w:6 * w]
    kd_ref[0], vd_ref[0], kf_ref[0], vf_ref[0] = kd, vd, kf, vf
    q_ref[0, :, :w] = (z[:, :w] * q_scale_d).astype(BF16)
    q_ref[0, :, w:] = (z[:, 3 * w:4 * w] * q_scale_f).astype(BF16)
    k_ref[0, :, :w] = kd.astype(BF16)
    k_ref[0, :, w:] = kf.astype(BF16)
    v_ref[0, :, :w] = vd.astype(BF16)
    v_ref[0, :, w:] = vf.astype(BF16)

    x = _dot(hn, wf_ref[...]) + bf_ref[...]
    lane = lax.broadcasted_iota(jnp.int32, x.shape, 1)
    lf = -(jnp.maximum(-x, 0.0) + jnp.log1p(jnp.exp(-jnp.abs(x))))
    lf = jnp.where(lane < N_HEADS, lf, 0.0)
    lf_ref[0] = lf[:, :N_HEADS]

    hi, mid, lo = _split3(lf * LOG2E)
    tri = tri_ref[...]
    c = (_dot(tri, hi.astype(BF16)) + _dot(tri, mid.astype(BF16)) + _dot(tri, lo.astype(BF16))
         + carry_ref[...])
    tm = c.shape[0]
    carry_ref[...] = c[tm - 1:tm, :]

    chi, cmid, clo = _split3(c)
    packed = (chi + pltpu.roll(cmid, N_HEADS, 1) + pltpu.roll(clo, 2 * N_HEADS, 1)
              + jnp.where(lane == 3 * N_HEADS, 1.0, 0.0)).astype(BF16)
    qaug_ref[0] = _dot(packed, selq_ref[...]).astype(BF16)
    kaug_ref[0] = _dot(packed, selk_ref[...]).astype(BF16)


def _aug_select_matrices():
    selq = np.zeros((LANES, N_PAIRS * LANES), np.float32)
    selk = np.zeros((LANES, N_PAIRS * LANES), np.float32)
    one = 3 * N_HEADS
    for p in range(N_PAIRS):
        for s in range(2):
            head = 2 * p + s
            base = p * LANES + s * AUG_PER_HEAD
            for piece in range(3):
                selq[piece * N_HEADS + head, base + piece] = 1.0
                selq[one, base + 3 + piece] = 1.0
                selk[one, base + piece] = 1.0
                selk[piece * N_HEADS + head, base + 3 + piece] = -1.0
    return jnp.asarray(selq, BF16), jnp.asarray(selk, BF16)


def _project(h, g, w_in, w_f, b_f, tm):
    b, t, d = h.shape
    w = HEADS_W
    selq, selk = _aug_select_matrices()
    tri = jnp.asarray(np.tril(np.ones((tm, tm), np.float32)), BF16)
    row = lambda width: pl.BlockSpec((1, tm, width), lambda i, j: (i, j, 0))
    sds = lambda width, dt: jax.ShapeDtypeStruct((b, t, width), dt)
    kern = functools.partial(_proj_kernel,
                             q_scale_d=DIFF_QK ** -0.5 * LOG2E, q_scale_f=HEAD_DIM ** -0.5 * LOG2E)
    return pl.pallas_call(
        kern,
        grid=(b, t // tm),
        in_specs=[row(d), _const_spec((1, d)), _const_spec(w_in.shape), _const_spec(w_f.shape),
                  _const_spec((1, LANES)), _const_spec((tm, tm)), _const_spec(selq.shape), _const_spec(selk.shape)],
        out_specs=[row(2 * w), row(2 * w), row(2 * w), row(w), row(w), row(w), row(w),
                   row(N_HEADS), row(N_PAIRS * LANES), row(N_PAIRS * LANES)],
        out_shape=[sds(2 * w, BF16), sds(2 * w, BF16), sds(2 * w, BF16),
                   sds(w, F32), sds(w, F32), sds(w, F32), sds(w, F32),
                   sds(N_HEADS, F32), sds(N_PAIRS * LANES, BF16), sds(N_PAIRS * LANES, BF16)],
        scratch_shapes=[pltpu.VMEM((1, LANES), F32)],
        compiler_params=pltpu.CompilerParams(dimension_semantics=("arbitrary", "arbitrary"),
                                             vmem_limit_bytes=VMEM_LIMIT),
        name="project",
    )(h, g, w_in, w_f, b_f, tri, selq, selk)


def _alibi_kernel(q_ref, k_ref):
    rows = q_ref.shape[0]
    pos = (pl.program_id(0) * rows + lax.broadcasted_iota(jnp.int32, (rows, LANES), 0)).astype(F32)
    lane = lax.broadcasted_iota(jnp.int32, (rows, LANES), 1)
    hi, mid, lo = _split3(pos * LOG2E)
    pieces = jnp.where(lane % 3 == 0, hi, jnp.where(lane % 3 == 1, mid, lo))
    q_ref[...] = jnp.where(lane < 3, -pieces, jnp.where(lane < 6, 1.0, 0.0)).astype(BF16)
    k_ref[...] = jnp.where(lane < 3, 1.0, jnp.where(lane < 6, pieces, 0.0)).astype(BF16)


def _alibi_tables(n, rows):
    return pl.pallas_call(
        _alibi_kernel,
        grid=(n // rows,),
        out_specs=[pl.BlockSpec((rows, LANES), lambda i: (i, 0))] * 2,
        out_shape=[jax.ShapeDtypeStruct((n, LANES), BF16)] * 2,
        name="alibi_tables",
    )()


def _alibi_slopes():
    slopes = 2.0 ** (-8.0 * np.arange(1, N_HEADS + 1, dtype=np.float64) / N_HEADS)
    assert np.all(np.log2(slopes) == np.round(np.log2(slopes))), "bias columns assume power-of-two slopes"
    return slopes.astype(np.float32)


def _lambda_full(lam_ref):
    l = lam_ref[...]
    s1 = jnp.sum(l[0:1] * l[1:2], axis=-1, keepdims=True)
    s2 = jnp.sum(l[2:3] * l[3:4], axis=-1, keepdims=True)
    return jnp.exp(s1) - jnp.exp(s2) + LAMBDA_INIT


def _head_rms(o, subln_row, lane):
    sq = o * o
    first = lane < HEAD_DIM
    sum_a = jnp.sum(jnp.where(first, sq, 0.0), axis=-1, keepdims=True)
    sum_b = jnp.sum(jnp.where(first, 0.0, sq), axis=-1, keepdims=True)
    ms = jnp.where(first, sum_a, sum_b) * (1.0 / HEAD_DIM)
    return o * lax.rsqrt(ms + RMS_EPS) * subln_row


def _attn_kernel(slope_ref, q_ref, k_ref, v_ref, qaug_ref, kaug_ref, lam_ref, subln_ref, o_ref,
                 qs_ref, m_ref, acc_ref, *, is_diff, tq, tk, rg, unroll):
    n_maps = 4 if is_diff else 2
    rows = n_maps * tq
    pair = pl.program_id(1)
    qi = pl.program_id(2)
    lane = lax.broadcasted_iota(jnp.int32, (tq, LANES), 1)

    q = q_ref[0].astype(F32)
    qa = qaug_ref[0].astype(F32)
    map_w = LANES // n_maps
    for m in range(n_maps):
        qm = jnp.where((lane >= m * map_w) & (lane < (m + 1) * map_w), q, 0.0)
        if is_diff:
            aug = qa * slope_ref[2 * pair + m // 2]
        else:
            aug = jnp.where((lane >= m * AUG_PER_HEAD) & (lane < (m + 1) * AUG_PER_HEAD), qa, 0.0)
        qs_ref[m * tq:(m + 1) * tq, :LANES] = qm.astype(BF16)
        qs_ref[m * tq:(m + 1) * tq, LANES:] = aug.astype(BF16)

    def step(j, width=1, diag=None):
        keys = width * tk
        start = pl.multiple_of(j * tk, tk)
        kfull = jnp.concatenate([k_ref[0, pl.ds(start, keys), :], kaug_ref[0, pl.ds(start, keys), :]], axis=1)
        vfull = jnp.concatenate([v_ref[0, pl.ds(start, keys), :], jnp.ones((keys, LANES), BF16)], axis=1)
        for r0 in range(0, rows, rg):
            rs = slice(r0, r0 + rg)
            kw = keys
            if diag is not None:
                kw = max(0, min(tk, r0 % tq + rg - diag * tk))
                if kw == 0:
                    continue
            s = _dot_nt(qs_ref[rs, :], kfull[:kw])
            if diag is not None and diag * tk + kw - 1 > r0 % tq:
                r = lax.broadcasted_iota(jnp.int32, (rg, kw), 0) + (r0 % tq + qi * tq)
                c = lax.broadcasted_iota(jnp.int32, (rg, kw), 1) + start
                s = jnp.where(c <= r, s, NEG_INF)
            m_cur = jnp.max(s, axis=1, keepdims=True)
            if diag == 0:
                m_new = jnp.broadcast_to(m_cur, (rg, LANES))
            else:
                m_prev = m_ref[rs, :]
                m_new = jnp.maximum(m_prev, m_cur)
            p = jnp.exp2((s - jnp.concatenate([m_new] * (kw // LANES), axis=1)).astype(BF16))
            pv = _dot(p, vfull[:kw])
            if diag == 0:
                acc_ref[rs, :] = pv
            else:
                alpha = jnp.exp2(m_prev - m_new)
                acc_ref[rs, :] = acc_ref[rs, :] * jnp.concatenate([alpha, alpha], axis=1) + pv
            m_ref[rs, :] = m_new

    n_full = (qi * tq) // tk
    for d in range(tq // tk):
        step(n_full + d, diag=d)

    def run(first, count):
        for u in range(count):
            step(first + u)

    def body(jj, carry):
        run(unroll * jj, unroll)
        return carry

    lax.fori_loop(0, n_full // unroll, body, 0)

    done = (n_full // unroll) * unroll
    part = unroll // 2
    while part >= 1:
        @pl.when((n_full - done) % (2 * part) >= part)
        def _(done=done, part=part):
            run(done, part)
        done = done + jnp.where((n_full - done) % (2 * part) >= part, part, 0)
        part //= 2

    def normalized(m):
        a = acc_ref[m * tq:(m + 1) * tq, :]
        return a[:, :LANES] / a[:, LANES:]

    first = lane < HEAD_DIM
    if is_diff:
        lam = _lambda_full(lam_ref)
        o = jnp.where(first, normalized(0) - lam * normalized(1), normalized(2) - lam * normalized(3))
        o = _head_rms(o, subln_ref[...], lane) * (1.0 - LAMBDA_INIT)
    else:
        o = jnp.where(first, normalized(0), normalized(1))
    o_ref[0] = o.astype(o_ref.dtype)


def _attend_prompt(q, k, v, qaug, kaug, slopes, lam, subln, *, is_diff, tq, tk, rg):
    b, t, _ = q.shape
    n_maps = 4 if is_diff else 2
    col0 = 0 if is_diff else N_PAIRS
    rows = n_maps * tq
    assert tq % tk == 0 and tq % rg == 0
    unroll = max(1, ATTN_CHAINS_PER_TRIP // (rows // rg))
    assert unroll & (unroll - 1) == 0
    if is_diff:
        qaug_spec = pl.BlockSpec((1, tq, LANES), lambda i, p, j, s: (0, j, 0))
        kaug_spec = pl.BlockSpec((1, t, LANES), lambda i, p, j, s: (0, 0, 0))
    else:
        qaug_spec = pl.BlockSpec((1, tq, LANES), lambda i, p, j, s: (i, j, p))
        kaug_spec = pl.BlockSpec((1, t, LANES), lambda i, p, j, s: (i, 0, p))
    grid_spec = pltpu.PrefetchScalarGridSpec(
        num_scalar_prefetch=1,
        grid=(b, N_PAIRS, t // tq),
        in_specs=[pl.BlockSpec((1, tq, LANES), lambda i, p, j, s: (i, j, col0 + p)),
                  pl.BlockSpec((1, t, LANES), lambda i, p, j, s: (i, 0, col0 + p)),
                  pl.BlockSpec((1, t, LANES), lambda i, p, j, s: (i, 0, col0 + p)),
                  qaug_spec, kaug_spec,
                  pl.BlockSpec(lam.shape, lambda i, p, j, s: (0, 0)),
                  pl.BlockSpec(subln.shape, lambda i, p, j, s: (0, 0))],
        out_specs=pl.BlockSpec((1, tq, LANES), lambda i, p, j, s: (i, j, p)),
        scratch_shapes=[pltpu.VMEM((rows, 2 * LANES), BF16),
                        pltpu.VMEM((rows, LANES), F32),
                        pltpu.VMEM((rows, 2 * LANES), F32)],
    )
    return pl.pallas_call(
        functools.partial(_attn_kernel, is_diff=is_diff, tq=tq, tk=tk, rg=rg, unroll=unroll),
        grid_spec=grid_spec,
        out_shape=jax.ShapeDtypeStruct((b, t, HEADS_W), BF16),
        compiler_params=pltpu.CompilerParams(dimension_semantics=("arbitrary",) * 3,
                                             vmem_limit_bytes=VMEM_LIMIT),
        name="attn_diff" if is_diff else "attn_fox",
    )(slopes, q, k, v, qaug, kaug, lam, subln)


def _decode_kernel(pt_ref, qd_ref, qf_ref, knd_ref, vnd_ref, knf_ref, vnf_ref, lfn_ref, slope_ref,
                   lam_ref, subln_ref, usuf_ref, *rest, pages_per_step, n_pages, page):
    g = pages_per_step
    dk_refs, dv_refs = rest[0:g], rest[g:2 * g]
    fk_refs, fv_refs, lf_refs = rest[2 * g:3 * g], rest[3 * g:4 * g], rest[4 * g:5 * g]
    od_ref, of_ref = rest[5 * g], rest[5 * g + 1]
    qbd_ref, qbf_ref, md_ref, ld_ref, accd_ref, mf_ref, lsum_ref, accf_ref, tail_ref = rest[5 * g + 2:]
    del pt_ref
    j = pl.program_id(1)
    nh = N_HEADS
    w = HEADS_W
    sub = lax.broadcasted_iota(jnp.int32, (2 * nh, w), 0)
    lane = lax.broadcasted_iota(jnp.int32, (2 * nh, w), 1)
    own = (lane // HEAD_DIM) == (sub % nh)
    own8 = own[:nh]

    @pl.when(j == 0)
    def _():
        half = ((lane % HEAD_DIM) < DIFF_QK) == (sub < nh)
        qbd = jnp.where(own & half, jnp.broadcast_to(qd_ref[...].astype(F32), (2 * nh, w)), 0.0)
        qbf = jnp.where(own8, jnp.broadcast_to(qf_ref[...].astype(F32), (nh, w)), 0.0)
        qbd_ref[...] = qbd.astype(BF16)
        qbf_ref[...] = qbf.astype(BF16)
        md_ref[...] = jnp.sum(qbd * knd_ref[...], axis=-1, keepdims=True)
        ld_ref[...] = jnp.ones_like(ld_ref)
        accd_ref[...] = jnp.broadcast_to(vnd_ref[...], (2 * nh, w))
        mf_ref[...] = jnp.sum(qbf * knf_ref[...], axis=-1, keepdims=True)
        lsum_ref[...] = jnp.ones_like(lsum_ref)
        accf_ref[...] = jnp.broadcast_to(vnf_ref[...], (nh, w))
        tail_ref[...] = jnp.zeros_like(tail_ref)

    past = n_pages * page
    pos_in_page = lax.broadcasted_iota(jnp.int32, (2 * nh, page), 1)
    slope2 = jnp.concatenate([slope_ref[...], slope_ref[...]], axis=0) * LOG2E
    lf_new = lfn_ref[...] * LOG2E
    usuf = usuf_ref[...]
    tail = tail_ref[...]
    sd, sf = [], []
    for i in range(g):
        logical = n_pages - 1 - (j * g + i)
        s = _dot(qbd_ref[...], dk_refs[i][...].astype(BF16))
        dist = (past - (logical * page + pos_in_page)).astype(F32)
        sd.append(s - slope2 * dist)
        s = _dot(qbf_ref[...], fk_refs[i][...].astype(BF16))
        hi, mid, lo = _split3(lf_refs[i][...] * LOG2E)
        suf = _dot(hi.astype(BF16), usuf) + _dot(mid.astype(BF16), usuf) + _dot(lo.astype(BF16), usuf)
        sf.append(s + suf[:, :page] + tail + lf_new)
        tail = tail + suf[:, page:]
    tail_ref[...] = tail

    def online(parts, m_ref, l_ref, acc_ref, v_refs):
        s = jnp.concatenate(parts, axis=1)
        m_prev = m_ref[...]
        m_new = jnp.maximum(m_prev, jnp.max(s, axis=-1, keepdims=True))
        p = jnp.exp2(s - m_new)
        alpha = jnp.exp2(m_prev - m_new)
        l_ref[...] = alpha * l_ref[...] + jnp.sum(p, axis=-1, keepdims=True)
        pv = None
        for i in range(g):
            part = _dot_nt(p[:, i * page:(i + 1) * page].astype(BF16), v_refs[i][...].astype(BF16))
            pv = part if pv is None else pv + part
        acc_ref[...] = alpha * acc_ref[...] + pv
        m_ref[...] = m_new

    online(sd, md_ref, ld_ref, accd_ref, dv_refs)
    online(sf, mf_ref, lsum_ref, accf_ref, fv_refs)

    @pl.when(j == pl.num_programs(1) - 1)
    def _():
        o_f = jnp.where(own8, accf_ref[...] / lsum_ref[...], 0.0)
        of_ref[...] = jnp.sum(o_f, axis=0, keepdims=True)
        lam = _lambda_full(lam_ref)
        a = accd_ref[...] / ld_ref[...]
        o = jnp.where(own8, a[:nh] - lam * a[nh:], 0.0)
        ms = jnp.sum(o * o, axis=-1, keepdims=True) * (1.0 / HEAD_DIM)
        o = o * lax.rsqrt(ms + RMS_EPS)
        od_ref[...] = jnp.sum(o, axis=0, keepdims=True) * subln_ref[...] * (1.0 - LAMBDA_INIT)


def _suffix_matrix(page):
    u = np.ones((page, 2 * page), np.float32)
    u[:, :page] = np.arange(page)[:, None] > np.arange(page)[None, :]
    return jnp.asarray(u, BF16)


def _attend_decode(qd, qf, knd, vnd, knf, vnf, lf_new, caches, page_table, slopes, lam, subln, pages_per_step):
    cdk, cdv, cfk, cfv, clf = caches
    nb, n_pages = page_table.shape
    page = cdk.shape[-1]
    assert page == LANES
    g = min(pages_per_step, n_pages)
    assert n_pages % g == 0
    usuf = _suffix_matrix(page)
    nh = N_HEADS
    w = HEADS_W

    def page_spec(i, rows):
        return pl.BlockSpec((None, rows, page),
                            lambda b, j, pt: (pt[b * n_pages + n_pages - 1 - (j * g + i)], 0, 0))

    per_b = lambda shape: pl.BlockSpec((None,) + shape, lambda b, j, pt: (b,) + (0,) * len(shape))
    const = lambda a: pl.BlockSpec(a.shape, lambda b, j, pt: (0,) * a.ndim)
    in_specs = ([per_b((1, w))] * 6 + [per_b((nh, 1)), const(slopes), const(lam), const(subln), const(usuf)]
                + [page_spec(i, w) for _ in range(4) for i in range(g)]
                + [page_spec(i, nh) for i in range(g)])
    scratch = [pltpu.VMEM((2 * nh, w), BF16), pltpu.VMEM((nh, w), BF16),
               pltpu.VMEM((2 * nh, 1), F32), pltpu.VMEM((2 * nh, 1), F32), pltpu.VMEM((2 * nh, w), F32),
               pltpu.VMEM((nh, 1), F32), pltpu.VMEM((nh, 1), F32), pltpu.VMEM((nh, w), F32),
               pltpu.VMEM((nh, page), F32)]
    grid_spec = pltpu.PrefetchScalarGridSpec(
        num_scalar_prefetch=1, grid=(nb, n_pages // g), in_specs=in_specs,
        out_specs=[per_b((1, w))] * 2, scratch_shapes=scratch)
    return pl.pallas_call(
        functools.partial(_decode_kernel, pages_per_step=g, n_pages=n_pages, page=page),
        grid_spec=grid_spec,
        out_shape=[jax.ShapeDtypeStruct((nb, 1, w), F32)] * 2,
        compiler_params=pltpu.CompilerParams(dimension_semantics=("arbitrary", "arbitrary"),
                                             vmem_limit_bytes=VMEM_LIMIT),
        name="attn_decode",
    )(page_table.reshape(-1), qd, qf, knd, vnd, knf, vnf, lf_new, slopes, lam, subln, usuf,
      *([cdk] * g + [cdv] * g + [cfk] * g + [cfv] * g + [clf] * g))


def _post_kernel(h_ref, od_ref, of_ref, p_ref, wo_ref, g2_ref, wg_ref, wu_ref, wd_ref, gp_ref, wpg_ref, bpg_ref,
                 wpp_ref, gf_ref, y_ref, *, chunk):
    w = od_ref.shape[1]
    h = h_ref[...] + _dot(od_ref[...], wo_ref[:w, :]) + _dot(of_ref[...], wo_ref[w:, :])
    hn = _rms(h, g2_ref[...]).astype(BF16)
    h = h + 0.5 * _swiglu(hn, wg_ref, wu_ref, wd_ref, chunk)
    hn = _rms(h, gp_ref[...]).astype(BF16)
    gate = jax.nn.sigmoid(_dot(hn, wpg_ref[...]) + bpg_ref[...])
    h = h + gate * _dot(p_ref[...].astype(BF16), wpp_ref[...])
    y_ref[...] = _rms(h, gf_ref[...])


def _post(h, o_d, o_f, p, wo, g2, wg, wu, wd, gp, wpg, bpg, wpp, gf, tm):
    n, d = h.shape
    row = lambda width: pl.BlockSpec((tm, width), lambda i: (i, 0))
    consts = (wo, g2, wg, wu, wd, gp, wpg, bpg, wpp, gf)
    return pl.pallas_call(
        functools.partial(_post_kernel, chunk=_ff_chunk(wg.shape[1])),
        grid=(n // tm,),
        in_specs=([row(d), row(o_d.shape[1]), row(o_f.shape[1]), row(p.shape[1])]
                  + [_const_spec(c.shape) for c in consts]),
        out_specs=row(d),
        out_shape=jax.ShapeDtypeStruct((n, d), F32),
        compiler_params=pltpu.CompilerParams(dimension_semantics=("arbitrary",), vmem_limit_bytes=VMEM_LIMIT),
        name="post",
    )(h, o_d, o_f, p, *consts)


def _row_tile(n, target):
    tm = min(n, target)
    assert n % tm == 0 and (tm % 8 == 0 or tm == n), (n, tm)
    return tm


def kernel(x_prompt, x_sample, cache_diff_k, cache_diff_v, cache_fox_k, cache_fox_v, cache_fox_logf, page_table, p_prompt, p_sample, ffn1_norm, ffn1_w_gate, ffn1_w_up, ffn1_w_down, mix_norm, w_in, b_forget, lambda_q1, lambda_k1, lambda_q2, lambda_k2, diff_subln, w_out, ffn2_norm, ffn2_w_gate, ffn2_w_up, ffn2_w_down, ple_norm, w_ple_gate, b_ple_gate, w_ple_proj, final_norm):
    depth = w_in.shape[0]
    assert depth == 1, "single-layer step only"
    d = x_prompt.shape[-1]
    w = HEADS_W
    assert w_in.shape[-1] == 6 * w + N_HEADS and w_out.shape[1:] == (2 * w, d)
    bf = lambda a: a.astype(BF16)
    row = lambda a: a.reshape(1, -1)

    wg1, wu1, wd1 = bf(ffn1_w_gate[0]), bf(ffn1_w_up[0]), bf(ffn1_w_down[0])
    wg2, wu2, wd2 = bf(ffn2_w_gate[0]), bf(ffn2_w_up[0]), bf(ffn2_w_down[0])
    w_qkv = bf(w_in[0, :, :6 * w])
    w_f = jnp.pad(bf(w_in[0, :, 6 * w:]), ((0, 0), (0, LANES - N_HEADS)))
    b_f = jnp.pad(b_forget[0], (0, LANES - N_HEADS)).reshape(1, LANES)
    wo, wpg, wpp = bf(w_out[0]), bf(w_ple_gate[0]), bf(w_ple_proj[0])
    lam = jnp.stack([lambda_q1[0], lambda_k1[0], lambda_q2[0], lambda_k2[0]])
    subln_pair = jnp.tile(diff_subln[0], 2).reshape(1, LANES)
    slopes = jnp.asarray(_alibi_slopes())

    def dense_pre(x3, tm):
        b, t, _ = x3.shape
        h1 = _ffn(x3.reshape(b * t, d), row(ffn1_norm[0]), wg1, wu1, wd1, tm)
        outs = _project(h1.reshape(b, t, d), row(mix_norm[0]), w_qkv, w_f, b_f, tm)
        return h1, outs

    def dense_post(h1, o_diff, o_fox, p2, tm):
        return _post(h1, o_diff, o_fox, p2, wo, row(ffn2_norm[0]), wg2, wu2, wd2, row(ple_norm[0]), wpg,
                     row(b_ple_gate[0]), wpp, row(final_norm), tm)

    def heads(a, b, t):
        return a.reshape(1, b, t, N_HEADS, HEAD_DIM)

    b, t, _ = x_prompt.shape
    tm = _row_tile(t, 512)
    h1, (q, k, v, kd, vd, kf, vf, lf, qaug, kaug) = dense_pre(x_prompt, tm)
    tq_d, tq_f = _row_tile(t, ATTN_ROWS_PER_STEP // 4), _row_tile(t, ATTN_ROWS_PER_STEP // 2)
    rg = min(tq_d, ATTN_ROW_GROUP)
    tk = min(tq_d, ATTN_KV_CHUNK)
    q_tab, k_tab = _alibi_tables(t, tq_d)
    o_d = _attend_prompt(q, k, v, q_tab[None], k_tab[None], slopes, lam, subln_pair, is_diff=True, tq=tq_d, tk=tk, rg=rg)
    o_f = _attend_prompt(q, k, v, qaug, kaug, slopes, lam, subln_pair, is_diff=False, tq=tq_f, tk=tk, rg=rg)
    y_prompt = dense_post(h1, o_d.reshape(b * t, w), o_f.reshape(b * t, w), p_prompt[0].reshape(b * t, -1),
                          tm).reshape(b, t, d)
    prompt_new = (heads(kd, b, t), heads(vd, b, t), heads(kf, b, t), heads(vf, b, t), lf.reshape(1, b, t, N_HEADS))

    nb, ts, _ = x_sample.shape
    assert ts == 1, "decode kernel handles one query per row"
    xs = x_sample.reshape(1, nb, d)
    h1s, (qs, _, _, kds, vds, kfs, vfs, lfs, _, _) = dense_pre(xs, nb)
    flat = lambda a: a.reshape(nb, 1, w)
    kv_view = lambda c: jnp.transpose(c[0], (0, 2, 3, 1)).reshape(c.shape[1], w, c.shape[2])
    caches = (kv_view(cache_diff_k), kv_view(cache_diff_v), kv_view(cache_fox_k), kv_view(cache_fox_v),
              jnp.transpose(cache_fox_logf[0], (0, 2, 1)))
    o_ds, o_fs = _attend_decode(flat(qs[0, :, :w]), flat(qs[0, :, w:]), flat(kds), flat(vds), flat(kfs), flat(vfs),
                                lfs.reshape(nb, N_HEADS, 1), caches, page_table, slopes.reshape(N_HEADS, 1),
                                lam, jnp.tile(diff_subln[0], N_HEADS).reshape(1, w),
                                pages_per_step=DECODE_PAGES_PER_STEP)
    y_sample = dense_post(h1s, bf(o_ds.reshape(nb, w)), bf(o_fs.reshape(nb, w)),
                          p_sample[0].reshape(nb, -1), nb).reshape(nb, 1, d)
    sample_new = (heads(kds, nb, 1), heads(vds, nb, 1), heads(kfs, nb, 1), heads(vfs, nb, 1),
                  lfs.reshape(1, nb, 1, N_HEADS))

    return (y_prompt, y_sample) + prompt_new + sample_new
```

```python
import functools
import math

import numpy as np
import jax
import jax.numpy as jnp
from jax import lax
from jax.experimental import pallas as pl
from jax.experimental.pallas import tpu as pltpu

F32 = jnp.float32
BF16 = jnp.bfloat16

HEAD_DIM = 64
N_HEADS = 8
HEADS_W = N_HEADS * HEAD_DIM
DIFF_QK = HEAD_DIM // 2
LANES = 128
N_PAIRS = HEADS_W // LANES
AUG_PER_HEAD = 6
RMS_EPS = 1e-6
LOG2E = 1.4426950408889634
LAMBDA_INIT = 0.8 - 0.6 * math.exp(-0.3 * 0)
NEG_INF = float("-inf")
VMEM_LIMIT = 56 * 1024 * 1024
DECODE_PAGES_PER_STEP = 16
ATTN_ROWS_PER_STEP = 2048
ATTN_ROW_GROUP = 256
ATTN_KV_CHUNK = 512
ATTN_CHAINS_PER_TRIP = 64


def _rms(x, g):
    return x * lax.rsqrt(jnp.mean(x * x, axis=-1, keepdims=True) + RMS_EPS) * g


def _dot(a, b):
    return jnp.dot(a, b, preferred_element_type=F32)


def _dot_nt(a, b):
    return lax.dot_general(a, b, (((1,), (1,)), ((), ())), preferred_element_type=F32)


def _split3(x):
    hi = x.astype(BF16).astype(F32)
    r = x - hi
    mid = r.astype(BF16).astype(F32)
    lo = (r - mid).astype(BF16).astype(F32)
    return hi, mid, lo


def _const_spec(shape):
    nd = len(shape)
    return pl.BlockSpec(shape, lambda *_: (0,) * nd, pipeline_mode=pl.Buffered(1))


def _swiglu(hn, wg_ref, wu_ref, wd_ref, chunk):
    d_ff = wg_ref.shape[1]
    out = None
    for c0 in range(0, d_ff, chunk):
        g = _dot(hn, wg_ref[:, c0:c0 + chunk])
        u = _dot(hn, wu_ref[:, c0:c0 + chunk])
        a = (g * jax.nn.sigmoid(g) * u).astype(BF16)
        part = _dot(a, wd_ref[c0:c0 + chunk, :])
        out = part if out is None else out + part
    return out


def _ff_chunk(d_ff):
    return d_ff


def _ffn_kernel(x_ref, g_ref, wg_ref, wu_ref, wd_ref, h_ref, *, chunk):
    x = x_ref[...]
    hn = _rms(x, g_ref[...]).astype(BF16)
    h_ref[...] = x + 0.5 * _swiglu(hn, wg_ref, wu_ref, wd_ref, chunk)


def _ffn(x, g, wg, wu, wd, tm):
    n, d = x.shape
    d_ff = wg.shape[1]
    return pl.pallas_call(
        functools.partial(_ffn_kernel, chunk=_ff_chunk(d_ff)),
        grid=(n // tm,),
        in_specs=[pl.BlockSpec((tm, d), lambda i: (i, 0)),
                  _const_spec((1, d)), _const_spec((d, d_ff)), _const_spec((d, d_ff)), _const_spec((d_ff, d))],
        out_specs=pl.BlockSpec((tm, d), lambda i: (i, 0)),
        out_shape=jax.ShapeDtypeStruct((n, d), F32),
        compiler_params=pltpu.CompilerParams(dimension_semantics=("arbitrary",), vmem_limit_bytes=VMEM_LIMIT),
        name="ffn1",
    )(x, g, wg, wu, wd)


def _proj_kernel(h_ref, g_ref, win_ref, wf_ref, bf_ref, tri_ref, selq_ref, selk_ref,
                 q_ref, k_ref, v_ref, kd_ref, vd_ref, kf_ref, vf_ref, lf_ref, qaug_ref, kaug_ref,
                 carry_ref, *, q_scale_d, q_scale_f):
    t = pl.program_id(1)
    w = HEADS_W

    @pl.when(t == 0)
    def _():
        carry_ref[...] = jnp.zeros_like(carry_ref)

    hn = _rms(h_ref[0], g_ref[...]).astype(BF16)
    z = _dot(hn, win_ref[...])
    kd, vd = z[:, w:2 * w], z[:, 2 * w:3 * w]
    kf, vf = z[:, 4 * w:5 * w], z[:, 5 * w:6 * w]
    kd_ref[0], vd_ref[0], kf_ref[0], vf_ref[0] = kd, vd, kf, vf
    q_ref[0, :, :w] = (z[:, :w] * q_scale_d).astype(BF16)
    q_ref[0, :, w:] = (z[:, 3 * w:4 * w] * q_scale_f).astype(BF16)
    k_ref[0, :, :w] = kd.astype(BF16)
    k_ref[0, :, w:] = kf.astype(BF16)
    v_ref[0, :, :w] = vd.astype(BF16)
    v_ref[0, :, w:] = vf.astype(BF16)

    x = _dot(hn, wf_ref[...]) + bf_ref[...]
    lane = lax.broadcasted_iota(jnp.int32, x.shape, 1)
    lf = -(jnp.maximum(-x, 0.0) + jnp.log1p(jnp.exp(-jnp.abs(x))))
    lf = jnp.where(lane < N_HEADS, lf, 0.0)
    lf_ref[0] = lf[:, :N_HEADS]

    hi, mid, lo = _split3(lf * LOG2E)
    tri = tri_ref[...]
    c = (_dot(tri, hi.astype(BF16)) + _dot(tri, mid.astype(BF16)) + _dot(tri, lo.astype(BF16))
         + carry_ref[...])
    tm = c.shape[0]
    carry_ref[...] = c[tm - 1:tm, :]

    chi, cmid, clo = _split3(c)
    packed = (chi + pltpu.roll(cmid, N_HEADS, 1) + pltpu.roll(clo, 2 * N_HEADS, 1)
              + jnp.where(lane == 3 * N_HEADS, 1.0, 0.0)).astype(BF16)
    qaug_ref[0] = _dot(packed, selq_ref[...]).astype(BF16)
    kaug_ref[0] = _dot(packed, selk_ref[...]).astype(BF16)


def _aug_select_matrices():
    selq = np.zeros((LANES, N_PAIRS * LANES), np.float32)
    selk = np.zeros((LANES, N_PAIRS * LANES), np.float32)
    one = 3 * N_HEADS
    for p in range(N_PAIRS):
        for s in range(2):
            head = 2 * p + s
            base = p * LANES + s * AUG_PER_HEAD
            for piece in range(3):
                selq[piece * N_HEADS + head, base + piece] = 1.0
                selq[one, base + 3 + piece] = 1.0
                selk[one, base + piece] = 1.0
                selk[piece * N_HEADS + head, base + 3 + piece] = -1.0
    return jnp.asarray(selq, BF16), jnp.asarray(selk, BF16)


def _project(h, g, w_in, w_f, b_f, tm):
    b, t, d = h.shape
    w = HEADS_W
    selq, selk = _aug_select_matrices()
    tri = jnp.asarray(np.tril(np.ones((tm, tm), np.float32)), BF16)
    row = lambda width: pl.BlockSpec((1, tm, width), lambda i, j: (i, j, 0))
    sds = lambda width, dt: jax.ShapeDtypeStruct((b, t, width), dt)
    kern = functools.partial(_proj_kernel,
                             q_scale_d=DIFF_QK ** -0.5 * LOG2E, q_scale_f=HEAD_DIM ** -0.5 * LOG2E)
    return pl.pallas_call(
        kern,
        grid=(b, t // tm),
        in_specs=[row(d), _const_spec((1, d)), _const_spec(w_in.shape), _const_spec(w_f.shape),
                  _const_spec((1, LANES)), _const_spec((tm, tm)), _const_spec(selq.shape), _const_spec(selk.shape)],
        out_specs=[row(2 * w), row(2 * w), row(2 * w), row(w), row(w), row(w), row(w),
                   row(N_HEADS), row(N_PAIRS * LANES), row(N_PAIRS * LANES)],
        out_shape=[sds(2 * w, BF16), sds(2 * w, BF16), sds(2 * w, BF16),
                   sds(w, F32), sds(w, F32), sds(w, F32), sds(w, F32),
                   sds(N_HEADS, F32), sds(N_PAIRS * LANES, BF16), sds(N_PAIRS * LANES, BF16)],
        scratch_shapes=[pltpu.VMEM((1, LANES), F32)],
        compiler_params=pltpu.CompilerParams(dimension_semantics=("arbitrary", "arbitrary"),
                                             vmem_limit_bytes=VMEM_LIMIT),
        name="project",
    )(h, g, w_in, w_f, b_f, tri, selq, selk)


def _alibi_kernel(q_ref, k_ref):
    rows = q_ref.shape[0]
    pos = (pl.program_id(0) * rows + lax.broadcasted_iota(jnp.int32, (rows, LANES), 0)).astype(F32)
    lane = lax.broadcasted_iota(jnp.int32, (rows, LANES), 1)
    hi, mid, lo = _split3(pos * LOG2E)
    pieces = jnp.where(lane % 3 == 0, hi, jnp.where(lane % 3 == 1, mid, lo))
    q_ref[...] = jnp.where(lane < 3, -pieces, jnp.where(lane < 6, 1.0, 0.0)).astype(BF16)
    k_ref[...] = jnp.where(lane < 3, 1.0, jnp.where(lane < 6, pieces, 0.0)).astype(BF16)


def _alibi_tables(n, rows):
    return pl.pallas_call(
        _alibi_kernel,
        grid=(n // rows,),
        out_specs=[pl.BlockSpec((rows, LANES), lambda i: (i, 0))] * 2,
        out_shape=[jax.ShapeDtypeStruct((n, LANES), BF16)] * 2,
        name="alibi_tables",
    )()


def _alibi_slopes():
    slopes = 2.0 ** (-8.0 * np.arange(1, N_HEADS + 1, dtype=np.float64) / N_HEADS)
    assert np.all(np.log2(slopes) == np.round(np.log2(slopes))), "bias columns assume power-of-two slopes"
    return slopes.astype(np.float32)


def _lambda_full(lam_ref):
    l = lam_ref[...]
    s1 = jnp.sum(l[0:1] * l[1:2], axis=-1, keepdims=True)
    s2 = jnp.sum(l[2:3] * l[3:4], axis=-1, keepdims=True)
    return jnp.exp(s1) - jnp.exp(s2) + LAMBDA_INIT


def _head_rms(o, subln_row, lane):
    sq = o * o
    first = lane < HEAD_DIM
    sum_a = jnp.sum(jnp.where(first, sq, 0.0), axis=-1, keepdims=True)
    sum_b = jnp.sum(jnp.where(first, 0.0, sq), axis=-1, keepdims=True)
    ms = jnp.where(first, sum_a, sum_b) * (1.0 / HEAD_DIM)
    return o * lax.rsqrt(ms + RMS_EPS) * subln_row


def _attn_kernel(slope_ref, q_ref, k_ref, v_ref, qaug_ref, kaug_ref, lam_ref, subln_ref, o_ref,
                 qs_ref, m_ref, acc_ref, *, is_diff, tq, tk, rg, unroll):
    n_maps = 4 if is_diff else 2
    rows = n_maps * tq
    pair = pl.program_id(1)
    qi = pl.program_id(2)
    lane = lax.broadcasted_iota(jnp.int32, (tq, LANES), 1)

    q = q_ref[0].astype(F32)
    qa = qaug_ref[0].astype(F32)
    map_w = LANES // n_maps
    for m in range(n_maps):
        qm = jnp.where((lane >= m * map_w) & (lane < (m + 1) * map_w), q, 0.0)
        if is_diff:
            aug = qa * slope_ref[2 * pair + m // 2]
        else:
            aug = jnp.where((lane >= m * AUG_PER_HEAD) & (lane < (m + 1) * AUG_PER_HEAD), qa, 0.0)
        qs_ref[m * tq:(m + 1) * tq, :LANES] = qm.astype(BF16)
        qs_ref[m * tq:(m + 1) * tq, LANES:] = aug.astype(BF16)

    def step(j, width=1, diag=None):
        keys = width * tk
        start = pl.multiple_of(j * tk, tk)
        kfull = jnp.concatenate([k_ref[0, pl.ds(start, keys), :], kaug_ref[0, pl.ds(start, keys), :]], axis=1)
        vfull = jnp.concatenate([v_ref[0, pl.ds(start, keys), :], jnp.ones((keys, LANES), BF16)], axis=1)
        for r0 in range(0, rows, rg):
            rs = slice(r0, r0 + rg)
            kw = keys
            if diag is not None:
                kw = max(0, min(tk, r0 % tq + rg - diag * tk))
                if kw == 0:
                    continue
            s = _dot_nt(qs_ref[rs, :], kfull[:kw])
            if diag is not None and diag * tk + kw - 1 > r0 % tq:
                r = lax.broadcasted_iota(jnp.int32, (rg, kw), 0) + (r0 % tq + qi * tq)
                c = lax.broadcasted_iota(jnp.int32, (rg, kw), 1) + start
                s = jnp.where(c <= r, s, NEG_INF)
            m_cur = jnp.max(s, axis=1, keepdims=True)
            if diag == 0:
                m_new = jnp.broadcast_to(m_cur, (rg, LANES))
            else:
                m_prev = m_ref[rs, :]
                m_new = jnp.maximum(m_prev, m_cur)
            p = jnp.exp2((s - jnp.concatenate([m_new] * (kw // LANES), axis=1)).astype(BF16))
            pv = _dot(p, vfull[:kw])
            if diag == 0:
                acc_ref[rs, :] = pv
            else:
                alpha = jnp.exp2(m_prev - m_new)
                acc_ref[rs, :] = acc_ref[rs, :] * jnp.concatenate([alpha, alpha], axis=1) + pv
            m_ref[rs, :] = m_new

    n_full = (qi * tq) // tk
    for d in range(tq // tk):
        step(n_full + d, diag=d)

    def run(first, count):
        for u in range(count):
            step(first + u)

    def body(jj, carry):
        run(unroll * jj, unroll)
        return carry

    lax.fori_loop(0, n_full // unroll, body, 0)

    done = (n_full // unroll) * unroll
    part = unroll // 2
    while part >= 1:
        @pl.when((n_full - done) % (2 * part) >= part)
        def _(done=done, part=part):
            run(done, part)
        done = done + jnp.where((n_full - done) % (2 * part) >= part, part, 0)
        part //= 2

    def normalized(m):
        a = acc_ref[m * tq:(m + 1) * tq, :]
        return a[:, :LANES] / a[:, LANES:]

    first = lane < HEAD_DIM
    if is_diff:
        lam = _lambda_full(lam_ref)
        o = jnp.where(first, normalized(0) - lam * normalized(1), normalized(2) - lam * normalized(3))
        o = _head_rms(o, subln_ref[...], lane) * (1.0 - LAMBDA_INIT)
    else:
        o = jnp.where(first, normalized(0), normalized(1))
    o_ref[0] = o.astype(o_ref.dtype)


def _attend_prompt(q, k, v, qaug, kaug, slopes, lam, subln, *, is_diff, tq, tk, rg):
    b, t, _ = q.shape
    n_maps = 4 if is_diff else 2
    col0 = 0 if is_diff else N_PAIRS
    rows = n_maps * tq
    assert tq % tk == 0 and tq % rg == 0
    unroll = max(1, ATTN_CHAINS_PER_TRIP // (rows // rg))
    assert unroll & (unroll - 1) == 0
    if is_diff:
        qaug_spec = pl.BlockSpec((1, tq, LANES), lambda i, p, j, s: (0, j, 0))
        kaug_spec = pl.BlockSpec((1, t, LANES), lambda i, p, j, s: (0, 0, 0))
    else:
        qaug_spec = pl.BlockSpec((1, tq, LANES), lambda i, p, j, s: (i, j, p))
        kaug_spec = pl.BlockSpec((1, t, LANES), lambda i, p, j, s: (i, 0, p))
    grid_spec = pltpu.PrefetchScalarGridSpec(
        num_scalar_prefetch=1,
        grid=(b, N_PAIRS, t // tq),
        in_specs=[pl.BlockSpec((1, tq, LANES), lambda i, p, j, s: (i, j, col0 + p)),
                  pl.BlockSpec((1, t, LANES), lambda i, p, j, s: (i, 0, col0 + p)),
                  pl.BlockSpec((1, t, LANES), lambda i, p, j, s: (i, 0, col0 + p)),
                  qaug_spec, kaug_spec,
                  pl.BlockSpec(lam.shape, lambda i, p, j, s: (0, 0)),
                  pl.BlockSpec(subln.shape, lambda i, p, j, s: (0, 0))],
        out_specs=pl.BlockSpec((1, tq, LANES), lambda i, p, j, s: (i, j, p)),
        scratch_shapes=[pltpu.VMEM((rows, 2 * LANES), BF16),
                        pltpu.VMEM((rows, LANES), F32),
                        pltpu.VMEM((rows, 2 * LANES), F32)],
    )
    return pl.pallas_call(
        functools.partial(_attn_kernel, is_diff=is_diff, tq=tq, tk=tk, rg=rg, unroll=unroll),
        grid_spec=grid_spec,
        out_shape=jax.ShapeDtypeStruct((b, t, HEADS_W), BF16),
        compiler_params=pltpu.CompilerParams(dimension_semantics=("arbitrary",) * 3,
                                             vmem_limit_bytes=VMEM_LIMIT),
        name="attn_diff" if is_diff else "attn_fox",
    )(slopes, q, k, v, qaug, kaug, lam, subln)


def _decode_kernel(pt_ref, qd_ref, qf_ref, knd_ref, vnd_ref, knf_ref, vnf_ref, lfn_ref, slope_ref,
                   lam_ref, subln_ref, usuf_ref, *rest, pages_per_step, n_pages, page):
    g = pages_per_step
    dk_refs, dv_refs = rest[0:g], rest[g:2 * g]
    fk_refs, fv_refs, lf_refs = rest[2 * g:3 * g], rest[3 * g:4 * g], rest[4 * g:5 * g]
    od_ref, of_ref = rest[5 * g], rest[5 * g + 1]
    qbd_ref, qbf_ref, md_ref, ld_ref, accd_ref, mf_ref, lsum_ref, accf_ref, tail_ref = rest[5 * g + 2:]
    del pt_ref
    j = pl.program_id(1)
    nh = N_HEADS
    w = HEADS_W
    sub = lax.broadcasted_iota(jnp.int32, (2 * nh, w), 0)
    lane = lax.broadcasted_iota(jnp.int32, (2 * nh, w), 1)
    own = (lane // HEAD_DIM) == (sub % nh)
    own8 = own[:nh]

    @pl.when(j == 0)
    def _():
        half = ((lane % HEAD_DIM) < DIFF_QK) == (sub < nh)
        qbd = jnp.where(own & half, jnp.broadcast_to(qd_ref[...].astype(F32), (2 * nh, w)), 0.0)
        qbf = jnp.where(own8, jnp.broadcast_to(qf_ref[...].astype(F32), (nh, w)), 0.0)
        qbd_ref[...] = qbd.astype(BF16)
        qbf_ref[...] = qbf.astype(BF16)
        md_ref[...] = jnp.sum(qbd * knd_ref[...], axis=-1, keepdims=True)
        ld_ref[...] = jnp.ones_like(ld_ref)
        accd_ref[...] = jnp.broadcast_to(vnd_ref[...], (2 * nh, w))
        mf_ref[...] = jnp.sum(qbf * knf_ref[...], axis=-1, keepdims=True)
        lsum_ref[...] = jnp.ones_like(lsum_ref)
        accf_ref[...] = jnp.broadcast_to(vnf_ref[...], (nh, w))
        tail_ref[...] = jnp.zeros_like(tail_ref)

    past = n_pages * page
    pos_in_page = lax.broadcasted_iota(jnp.int32, (2 * nh, page), 1)
    slope2 = jnp.concatenate([slope_ref[...], slope_ref[...]], axis=0) * LOG2E
    lf_new = lfn_ref[...] * LOG2E
    usuf = usuf_ref[...]
    tail = tail_ref[...]
    sd, sf = [], []
    for i in range(g):
        logical = n_pages - 1 - (j * g + i)
        s = _dot(qbd_ref[...], dk_refs[i][...].astype(BF16))
        dist = (past - (logical * page + pos_in_page)).astype(F32)
        sd.append(s - slope2 * dist)
        s = _dot(qbf_ref[...], fk_refs[i][...].astype(BF16))
        hi, mid, lo = _split3(lf_refs[i][...] * LOG2E)
        suf = _dot(hi.astype(BF16), usuf) + _dot(mid.astype(BF16), usuf) + _dot(lo.astype(BF16), usuf)
        sf.append(s + suf[:, :page] + tail + lf_new)
        tail = tail + suf[:, page:]
    tail_ref[...] = tail

    def online(parts, m_ref, l_ref, acc_ref, v_refs):
        s = jnp.concatenate(parts, axis=1)
        m_prev = m_ref[...]
        m_new = jnp.maximum(m_prev, jnp.max(s, axis=-1, keepdims=True))
        p = jnp.exp2(s - m_new)
        alpha = jnp.exp2(m_prev - m_new)
        l_ref[...] = alpha * l_ref[...] + jnp.sum(p, axis=-1, keepdims=True)
        pv = None
        for i in range(g):
            part = _dot_nt(p[:, i * page:(i + 1) * page].astype(BF16), v_refs[i][...].astype(BF16))
            pv = part if pv is None else pv + part
        acc_ref[...] = alpha * acc_ref[...] + pv
        m_ref[...] = m_new

    online(sd, md_ref, ld_ref, accd_ref, dv_refs)
    online(sf, mf_ref, lsum_ref, accf_ref, fv_refs)

    @pl.when(j == pl.num_programs(1) - 1)
    def _():
        o_f = jnp.where(own8, accf_ref[...] / lsum_ref[...], 0.0)
        of_ref[...] = jnp.sum(o_f, axis=0, keepdims=True)
        lam = _lambda_full(lam_ref)
        a = accd_ref[...] / ld_ref[...]
        o = jnp.where(own8, a[:nh] - lam * a[nh:], 0.0)
        ms = jnp.sum(o * o, axis=-1, keepdims=True) * (1.0 / HEAD_DIM)
        o = o * lax.rsqrt(ms + RMS_EPS)
        od_ref[...] = jnp.sum(o, axis=0, keepdims=True) * subln_ref[...] * (1.0 - LAMBDA_INIT)


def _suffix_matrix(page):
    u = np.ones((page, 2 * page), np.float32)
    u[:, :page] = np.arange(page)[:, None] > np.arange(page)[None, :]
    return jnp.asarray(u, BF16)


def _attend_decode(qd, qf, knd, vnd, knf, vnf, lf_new, caches, page_table, slopes, lam, subln, pages_per_step):
    cdk, cdv, cfk, cfv, clf = caches
    nb, n_pages = page_table.shape
    page = cdk.shape[-1]
    assert page == LANES
    g = min(pages_per_step, n_pages)
    assert n_pages % g == 0
    usuf = _suffix_matrix(page)
    nh = N_HEADS
    w = HEADS_W

    def page_spec(i, rows):
        return pl.BlockSpec((None, rows, page),
                            lambda b, j, pt: (pt[b * n_pages + n_pages - 1 - (j * g + i)], 0, 0))

    per_b = lambda shape: pl.BlockSpec((None,) + shape, lambda b, j, pt: (b,) + (0,) * len(shape))
    const = lambda a: pl.BlockSpec(a.shape, lambda b, j, pt: (0,) * a.ndim)
    in_specs = ([per_b((1, w))] * 6 + [per_b((nh, 1)), const(slopes), const(lam), const(subln), const(usuf)]
                + [page_spec(i, w) for _ in range(4) for i in range(g)]
                + [page_spec(i, nh) for i in range(g)])
    scratch = [pltpu.VMEM((2 * nh, w), BF16), pltpu.VMEM((nh, w), BF16),
               pltpu.VMEM((2 * nh, 1), F32), pltpu.VMEM((2 * nh, 1), F32), pltpu.VMEM((2 * nh, w), F32),
               pltpu.VMEM((nh, 1), F32), pltpu.VMEM((nh, 1), F32), pltpu.VMEM((nh, w), F32),
               pltpu.VMEM((nh, page), F32)]
    grid_spec = pltpu.PrefetchScalarGridSpec(
        num_scalar_prefetch=1, grid=(nb, n_pages // g), in_specs=in_specs,
        out_specs=[per_b((1, w))] * 2, scratch_shapes=scratch)
    return pl.pallas_call(
        functools.partial(_decode_kernel, pages_per_step=g, n_pages=n_pages, page=page),
        grid_spec=grid_spec,
        out_shape=[jax.ShapeDtypeStruct((nb, 1, w), F32)] * 2,
        compiler_params=pltpu.CompilerParams(dimension_semantics=("arbitrary", "arbitrary"),
                                             vmem_limit_bytes=VMEM_LIMIT),
        name="attn_decode",
    )(page_table.reshape(-1), qd, qf, knd, vnd, knf, vnf, lf_new, slopes, lam, subln, usuf,
      *([cdk] * g + [cdv] * g + [cfk] * g + [cfv] * g + [clf] * g))


def _post_kernel(h_ref, od_ref, of_ref, p_ref, wo_ref, g2_ref, wg_ref, wu_ref, wd_ref, gp_ref, wpg_ref, bpg_ref,
                 wpp_ref, gf_ref, y_ref, *, chunk):
    w = od_ref.shape[1]
    h = h_ref[...] + _dot(od_ref[...], wo_ref[:w, :]) + _dot(of_ref[...], wo_ref[w:, :])
    hn = _rms(h, g2_ref[...]).astype(BF16)
    h = h + 0.5 * _swiglu(hn, wg_ref, wu_ref, wd_ref, chunk)
    hn = _rms(h, gp_ref[...]).astype(BF16)
    gate = jax.nn.sigmoid(_dot(hn, wpg_ref[...]) + bpg_ref[...])
    h = h + gate * _dot(p_ref[...].astype(BF16), wpp_ref[...])
    y_ref[...] = _rms(h, gf_ref[...])


def _post(h, o_d, o_f, p, wo, g2, wg, wu, wd, gp, wpg, bpg, wpp, gf, tm):
    n, d = h.shape
    row = lambda width: pl.BlockSpec((tm, width), lambda i: (i, 0))
    consts = (wo, g2, wg, wu, wd, gp, wpg, bpg, wpp, gf)
    return pl.pallas_call(
        functools.partial(_post_kernel, chunk=_ff_chunk(wg.shape[1])),
        grid=(n // tm,),
        in_specs=([row(d), row(o_d.shape[1]), row(o_f.shape[1]), row(p.shape[1])]
                  + [_const_spec(c.shape) for c in consts]),
        out_specs=row(d),
        out_shape=jax.ShapeDtypeStruct((n, d), F32),
        compiler_params=pltpu.CompilerParams(dimension_semantics=("arbitrary",), vmem_limit_bytes=VMEM_LIMIT),
        name="post",
    )(h, o_d, o_f, p, *consts)


def _row_tile(n, target):
    tm = min(n, target)
    assert n % tm == 0 and (tm % 8 == 0 or tm == n), (n, tm)
    return tm


def kernel(x_prompt, x_sample, cache_diff_k, cache_diff_v, cache_fox_k, cache_fox_v, cache_fox_logf, page_table, p_prompt, p_sample, ffn1_norm, ffn1_w_gate, ffn1_w_up, ffn1_w_down, mix_norm, w_in, b_forget, lambda_q1, lambda_k1, lambda_q2, lambda_k2, diff_subln, w_out, ffn2_norm, ffn2_w_gate, ffn2_w_up, ffn2_w_down, ple_norm, w_ple_gate, b_ple_gate, w_ple_proj, final_norm):
    depth = w_in.shape[0]
    assert depth == 1, "single-layer step only"
    d = x_prompt.shape[-1]
    w = HEADS_W
    assert w_in.shape[-1] == 6 * w + N_HEADS and w_out.shape[1:] == (2 * w, d)
    bf = lambda a: a.astype(BF16)
    row = lambda a: a.reshape(1, -1)

    wg1, wu1, wd1 = bf(ffn1_w_gate[0]), bf(ffn1_w_up[0]), bf(ffn1_w_down[0])
    wg2, wu2, wd2 = bf(ffn2_w_gate[0]), bf(ffn2_w_up[0]), bf(ffn2_w_down[0])
    w_qkv = bf(w_in[0, :, :6 * w])
    w_f = jnp.pad(bf(w_in[0, :, 6 * w:]), ((0, 0), (0, LANES - N_HEADS)))
    b_f = jnp.pad(b_forget[0], (0, LANES - N_HEADS)).reshape(1, LANES)
    wo, wpg, wpp = bf(w_out[0]), bf(w_ple_gate[0]), bf(w_ple_proj[0])
    lam = jnp.stack([lambda_q1[0], lambda_k1[0], lambda_q2[0], lambda_k2[0]])
    subln_pair = jnp.tile(diff_subln[0], 2).reshape(1, LANES)
    slopes = jnp.asarray(_alibi_slopes())

    def dense_pre(x3, tm):
        b, t, _ = x3.shape
        h1 = _ffn(x3.reshape(b * t, d), row(ffn1_norm[0]), wg1, wu1, wd1, tm)
        outs = _project(h1.reshape(b, t, d), row(mix_norm[0]), w_qkv, w_f, b_f, tm)
        return h1, outs

    def dense_post(h1, o_diff, o_fox, p2, tm):
        return _post(h1, o_diff, o_fox, p2, wo, row(ffn2_norm[0]), wg2, wu2, wd2, row(ple_norm[0]), wpg,
                     row(b_ple_gate[0]), wpp, row(final_norm), tm)

    def heads(a, b, t):
        return a.reshape(1, b, t, N_HEADS, HEAD_DIM)

    b, t, _ = x_prompt.shape
    tm = _row_tile(t, 512)
    h1, (q, k, v, kd, vd, kf, vf, lf, qaug, kaug) = dense_pre(x_prompt, tm)
    tq_d, tq_f = _row_tile(t, ATTN_ROWS_PER_STEP // 4), _row_tile(t, ATTN_ROWS_PER_STEP // 2)
    rg = min(tq_d, ATTN_ROW_GROUP)
    tk = min(tq_d, ATTN_KV_CHUNK)
    q_tab, k_tab = _alibi_tables(t, tq_d)
    o_d = _attend_prompt(q, k, v, q_tab[None], k_tab[None], slopes, lam, subln_pair, is_diff=True, tq=tq_d, tk=tk, rg=rg)
    o_f = _attend_prompt(q, k, v, qaug, kaug, slopes, lam, subln_pair, is_diff=False, tq=tq_f, tk=tk, rg=rg)
    y_prompt = dense_post(h1, o_d.reshape(b * t, w), o_f.reshape(b * t, w), p_prompt[0].reshape(b * t, -1),
                          tm).reshape(b, t, d)
    prompt_new = (heads(kd, b, t), heads(vd, b, t), heads(kf, b, t), heads(vf, b, t), lf.reshape(1, b, t, N_HEADS))

    nb, ts, _ = x_sample.shape
    assert ts == 1, "decode kernel handles one query per row"
    xs = x_sample.reshape(1, nb, d)
    h1s, (qs, _, _, kds, vds, kfs, vfs, lfs, _, _) = dense_pre(xs, nb)
    flat = lambda a: a.reshape(nb, 1, w)
    kv_view = lambda c: jnp.transpose(c[0], (0, 2, 3, 1)).reshape(c.shape[1], w, c.shape[2])
    caches = (kv_view(cache_diff_k), kv_view(cache_diff_v), kv_view(cache_fox_k), kv_view(cache_fox_v),
              jnp.transpose(cache_fox_logf[0], (0, 2, 1)))
    o_ds, o_fs = _attend_decode(flat(qs[0, :, :w]), flat(qs[0, :, w:]), flat(kds), flat(vds), flat(kfs), flat(vfs),
                                lfs.reshape(nb, N_HEADS, 1), caches, page_table, slopes.reshape(N_HEADS, 1),
                                lam, jnp.tile(diff_subln[0], N_HEADS).reshape(1, w),
                                pages_per_step=DECODE_PAGES_PER_STEP)
    y_sample = dense_post(h1s, bf(o_ds.reshape(nb, w)), bf(o_fs.reshape(nb, w)),
                          p_sample[0].reshape(nb, -1), nb).reshape(nb, 1, d)
    sample_new = (heads(kds, nb, 1), heads(vds, nb, 1), heads(kfs, nb, 1), heads(vfs, nb, 1),
                  lfs.reshape(1, nb, 1, N_HEADS))

    return (y_prompt, y_sample) + prompt_new + sample_new
```
